```python
import jax, jax.numpy as jnp
from jax import lax
import numpy as np

D_MODEL = 4096
BATCH = 8
SEQ = 2048
DEPTH = 2

HEAD_DIM = 128
N_Q_HEADS = D_MODEL // 256
N_KV_HEADS = N_Q_HEADS // 4
Q_PER_KV = N_Q_HEADS // N_KV_HEADS
WINDOW = 128
BLOCK = 128
ATTN_WIDTH = N_Q_HEADS * HEAD_DIM
KV_WIDTH = N_KV_HEADS * HEAD_DIM
NEG_INF = -1e30
SSM_WIDTH = D_MODEL // 4
SSM_GROUP = 16
SSM_GROUPS = SSM_WIDTH // SSM_GROUP
SSM_STATE = 64
DT_MIN = 1e-3
DT_MAX = 1e-1
IN_COLS = ATTN_WIDTH + 2 * KV_WIDTH + SSM_WIDTH + 2 * D_MODEL
SPLITS = [ATTN_WIDTH, ATTN_WIDTH + KV_WIDTH, ATTN_WIDTH + 2 * KV_WIDTH,
          ATTN_WIDTH + 2 * KV_WIDTH + SSM_WIDTH, ATTN_WIDTH + 2 * KV_WIDTH + SSM_WIDTH + D_MODEL]
PEER_HEADS = 8
PEER_DK = 256
N_KEYS = 128
N_EXPERTS = N_KEYS * N_KEYS
PEER_TOPK = 16
PEER_CHUNK = 128
NORM_EPS = 1e-6

kernel_name = 'hybrid_swa_s5_peer_gated_block'


def rms_norm(x, g):
    xf = x.astype(jnp.float32)
    y = xf * lax.rsqrt(jnp.mean(xf * xf, axis=-1, keepdims=True) + NORM_EPS)
    return (y * g.astype(jnp.float32)).astype(x.dtype)


def alibi_slopes():
    return jnp.asarray(2.0 ** (-8.0 * np.arange(1, N_Q_HEADS + 1) / N_Q_HEADS), dtype=jnp.float32)


def sliding_window_attention(q, k, v, sinks):
    b, s = q.shape[:2]
    nb = s // BLOCK
    qb = q.reshape(b, nb, BLOCK, N_KV_HEADS, Q_PER_KV, HEAD_DIM)
    kb = k.reshape(b, nb, BLOCK, N_KV_HEADS, HEAD_DIM)
    vb = v.reshape(b, nb, BLOCK, N_KV_HEADS, HEAD_DIM)
    pad = jnp.zeros_like(kb[:, :1])
    k_win = jnp.concatenate([jnp.concatenate([pad, kb[:, :-1]], axis=1), kb], axis=2)
    v_win = jnp.concatenate([jnp.concatenate([pad, vb[:, :-1]], axis=1), vb], axis=2)
    logits = jnp.einsum('bnqkgd,bnskd->bnkgqs', qb, k_win,
                        preferred_element_type=jnp.float32) * (HEAD_DIM ** -0.5)
    qi = jnp.arange(BLOCK)[:, None]
    sj = jnp.arange(2 * BLOCK)[None, :]
    dist = qi - sj + BLOCK
    blk = jnp.arange(nb)[:, None, None]
    valid = (dist >= 0) & (dist < WINDOW) & ((blk > 0) | (sj >= BLOCK))
    slopes = alibi_slopes().reshape(N_KV_HEADS, Q_PER_KV)
    bias = -slopes[:, :, None, None] * dist.astype(jnp.float32)
    logits = jnp.where(valid[None, :, None, None], logits + bias[None, None], NEG_INF)
    sink = sinks.astype(jnp.float32).reshape(N_KV_HEADS, Q_PER_KV)[None, None, :, :, None, None]
    sink = jnp.broadcast_to(sink, logits.shape[:-1] + (1,))
    probs = jax.nn.softmax(jnp.concatenate([logits, sink], axis=-1), axis=-1)[..., :-1]
    out = jnp.einsum('bnkgqs,bnskd->bnqkgd', probs.astype(v.dtype), v_win)
    return out.reshape(b, s, ATTN_WIDTH)


def _complex_linear_combine(c1, c2):
    a1r, a1i, b1r, b1i = c1
    a2r, a2i, b2r, b2i = c2
    return (a2r * a1r - a2i * a1i,
            a2r * a1i + a2i * a1r,
            a2r * b1r - a2i * b1i + b2r,
            a2r * b1i + a2i * b1r + b2i)


def s5_mixer(u, a_re, a_im, log_step, b_re, b_im, c_re, c_im, d_skip, w_glu):
    bsz, s, _ = u.shape
    f32 = jnp.float32
    uf = u.astype(f32).reshape(bsz, s, SSM_GROUPS, SSM_GROUP)
    ar = a_re.astype(f32)
    ai = a_im.astype(f32)
    dt = jnp.exp(log_step.astype(f32))[:, None]
    mag = jnp.exp(ar * dt)
    lr = mag * jnp.cos(ai * dt)
    li = mag * jnp.sin(ai * dt)
    den = ar * ar + ai * ai
    nr = lr - 1.0
    zr = (nr * ar + li * ai) / den
    zi = (li * ar - nr * ai) / den
    br = b_re.astype(f32)
    bi = b_im.astype(f32)
    bbar_r = zr[..., None] * br - zi[..., None] * bi
    bbar_i = zr[..., None] * bi + zi[..., None] * br
    bu_r = jnp.einsum('bsgh,gph->bsgp', uf, bbar_r)
    bu_i = jnp.einsum('bsgh,gph->bsgp', uf, bbar_i)
    lam_r = jnp.broadcast_to(lr[None, None], bu_r.shape)
    lam_i = jnp.broadcast_to(li[None, None], bu_i.shape)
    _, _, xr, xi = lax.associative_scan(_complex_linear_combine, (lam_r, lam_i, bu_r, bu_i), axis=1)
    y = (jnp.einsum('bsgp,ghp->bsgh', xr, c_re.astype(f32))
         - jnp.einsum('bsgp,ghp->bsgh', xi, c_im.astype(f32)))
    y = y.reshape(bsz, s, SSM_WIDTH) + d_skip.astype(f32) * u.astype(f32)
    z = jax.nn.gelu(y)
    gl = z @ w_glu.astype(f32)
    out = gl[..., :SSM_WIDTH] * jax.nn.sigmoid(gl[..., SSM_WIDTH:])
    return out.astype(u.dtype)


def peer_ffn(xn, w_q, keys, u_tab, v_tab):
    bsz, s, d = xn.shape
    q = (xn @ w_q).astype(jnp.float32).reshape(bsz, s, PEER_HEADS, 2, PEER_DK // 2)
    sub = jnp.einsum('bshcd,hcnd->bshcn', q, keys.astype(jnp.float32))
    sv, si = lax.top_k(sub, PEER_TOPK)
    cand = (sv[..., 0, :, None] + sv[..., 1, None, :]).reshape(bsz, s, PEER_HEADS, PEER_TOPK * PEER_TOPK)
    cidx = (si[..., 0, :, None] * N_KEYS + si[..., 1, None, :]).reshape(bsz, s, PEER_HEADS, PEER_TOPK * PEER_TOPK)
    top_s, pos = lax.top_k(cand, PEER_TOPK)
    expert = jnp.take_along_axis(cidx, pos, axis=-1)
    gate = jax.nn.softmax(top_s, axis=-1)
    n_chunks = (bsz * s) // PEER_CHUNK
    xs = xn.reshape(n_chunks, PEER_CHUNK, d)
    es = expert.reshape(n_chunks, PEER_CHUNK, PEER_HEADS * PEER_TOPK)
    gs = gate.astype(xn.dtype).reshape(n_chunks, PEER_CHUNK, PEER_HEADS * PEER_TOPK)

    def chunk(args):
        xc, ec, gc = args
        uc = jnp.take(u_tab, ec, axis=0)
        act = jax.nn.gelu(jnp.einsum('ckd,cd->ck', uc, xc)) * gc
        vc = jnp.take(v_tab, ec, axis=0)
        return jnp.einsum('ck,ckd->cd', act, vc)

    out = lax.map(chunk, (xs, es, gs))
    return out.reshape(bsz, s, d)


def setup_inputs(seed: int = 0) -> dict:
    key = jax.random.key(seed)
    ks = jax.random.split(key, 24)
    f32 = jnp.float32

    def nrm(k, shape, scale):
        return jax.random.normal(k, shape, f32) * scale

    L, D, G, P, H = DEPTH, D_MODEL, SSM_GROUPS, SSM_STATE, SSM_GROUP
    return {
        'x': nrm(ks[0], (BATCH, SEQ, D), 1.0),
        'ln_mix': 1.0 + nrm(ks[1], (L, D), 0.02),
        'w_in': nrm(ks[2], (L, D, IN_COLS), D ** -0.5),
        'ssm_a_re': -0.5 * jnp.exp(nrm(ks[3], (L, G, P), 0.02)),
        'ssm_a_im': jnp.pi * jnp.arange(P, dtype=f32)[None, None, :] + nrm(ks[4], (L, G, P), 0.02),
        'ssm_log_step': jax.random.uniform(ks[5], (L, G), f32, float(np.log(DT_MIN)), float(np.log(DT_MAX))),
        'ssm_b_re': nrm(ks[6], (L, G, P, H), (2 * H) ** -0.5),
        'ssm_b_im': nrm(ks[7], (L, G, P, H), (2 * H) ** -0.5),
        'ssm_c_re': nrm(ks[8], (L, G, H, P), (2 * P) ** -0.5),
        'ssm_c_im': nrm(ks[9], (L, G, H, P), (2 * P) ** -0.5),
        'ssm_d': nrm(ks[10], (L, SSM_WIDTH), 1.0),
        'w_glu': nrm(ks[11], (L, SSM_WIDTH, 2 * SSM_WIDTH), SSM_WIDTH ** -0.5),
        'attn_sinks': nrm(ks[12], (L, N_Q_HEADS), 0.5),
        'w_attn_branch': nrm(ks[13], (L, ATTN_WIDTH, D), ATTN_WIDTH ** -0.5),
        'w_ssm_branch': nrm(ks[14], (L, SSM_WIDTH, D), SSM_WIDTH ** -0.5),
        'w_out': nrm(ks[15], (L, D, D), D ** -0.5),
        'ln_ffn': 1.0 + nrm(ks[16], (L, D), 0.02),
        'w_peer_q': nrm(ks[17], (L, D, PEER_HEADS * PEER_DK), D ** -0.5),
        'peer_keys': nrm(ks[18], (L, PEER_HEADS, 2, N_KEYS, PEER_DK // 2), (PEER_DK // 2) ** -0.5),
        'peer_u': nrm(ks[19], (L, N_EXPERTS, D), D ** -0.5),
        'peer_v': nrm(ks[20], (L, N_EXPERTS, D), PEER_HEADS ** -0.5),
        'ln_final': 1.0 + nrm(ks[21], (D,), 0.02),
    }


def reference(x, ln_mix, w_in, ssm_a_re, ssm_a_im, ssm_log_step, ssm_b_re, ssm_b_im,
              ssm_c_re, ssm_c_im, ssm_d, w_glu, attn_sinks, w_attn_branch, w_ssm_branch,
              w_out, ln_ffn, w_peer_q, peer_keys, peer_u, peer_v, ln_final):
    bsz, s, _ = x.shape
    for l in range(DEPTH):
        xn = rms_norm(x, ln_mix[l])
        proj = xn @ w_in[l]
        q, k, v, u, g_attn, g_ssm = jnp.split(proj, SPLITS, axis=-1)
        attn = sliding_window_attention(q.reshape(bsz, s, N_Q_HEADS, HEAD_DIM),
                                        k.reshape(bsz, s, N_KV_HEADS, HEAD_DIM),
                                        v.reshape(bsz, s, N_KV_HEADS, HEAD_DIM),
                                        attn_sinks[l])
        ssm = s5_mixer(u, ssm_a_re[l], ssm_a_im[l], ssm_log_step[l], ssm_b_re[l], ssm_b_im[l],
                       ssm_c_re[l], ssm_c_im[l], ssm_d[l], w_glu[l])
        merged = (jax.nn.sigmoid(g_attn) * (attn @ w_attn_branch[l])
                  + jax.nn.sigmoid(g_ssm) * (ssm @ w_ssm_branch[l]))
        x = x + merged @ w_out[l]
        x = x + peer_ffn(rms_norm(x, ln_ffn[l]), w_peer_q[l], peer_keys[l], peer_u[l], peer_v[l])
    return rms_norm(x, ln_final)
```

```python
import functools

import numpy as np
import jax
import jax.numpy as jnp
from jax import lax
from jax.experimental import pallas as pl
from jax.experimental.pallas import tpu as pltpu

F32 = jnp.float32
BF16 = jnp.bfloat16

HEAD_DIM = 128
N_Q_HEADS = 16
N_KV_HEADS = 4
Q_PER_KV = 4
WINDOW = 128
BLOCK = 128
NEG_INF = -1e30
SSM_GROUP = 16
SSM_STATE = 64
PEER_HEADS = 8
N_KEYS = 128
PEER_TOPK = 16
NORM_EPS = 1e-6

V7X_VMEM_BYTES = 64 * 1024 * 1024
VMEM_LIMIT = V7X_VMEM_BYTES - 8 * 1024 * 1024
SUBLANES = 8
LANES = 128


def _params(*sem):
    return pltpu.CompilerParams(dimension_semantics=sem, vmem_limit_bytes=VMEM_LIMIT)


def _addnorm_kernel(*refs, n_in, emit_sum, emit_norm_f32):
    ins = refs[:n_in]
    g_ref = refs[n_in]
    outs = refs[n_in + 1:]
    x = ins[0][...]
    if n_in == 2:
        x = x + ins[1][...]
    y = x * lax.rsqrt(jnp.mean(x * x, axis=-1, keepdims=True) + NORM_EPS)
    y = y * g_ref[...]
    k = 0
    if emit_sum:
        outs[k][...] = x
        k += 1
    if emit_norm_f32:
        outs[k][...] = y
    else:
        outs[k][...] = y.astype(BF16)


def _addnorm(inputs, gain, *, bsz, seq, in_batch_major=False, out_batch_major=False,
             emit_sum=False, emit_norm_f32=False, rows=256):
    d = inputs[0].shape[-1]
    rows = min(rows, seq)
    grid = (seq // rows, bsz)

    def tm_view(a):
        return a.reshape(seq, bsz * d)

    def bm_spec():
        return pl.BlockSpec((rows, d), lambda i, b: (b * (seq // rows) + i, 0))

    def tm_spec():
        return pl.BlockSpec((rows, d), lambda i, b: (i, b))

    if in_batch_major:
        ins = [a.reshape(bsz * seq, d) for a in inputs]
        in_specs = [bm_spec() for _ in ins]
    else:
        ins = [tm_view(a) for a in inputs]
        in_specs = [tm_spec() for _ in ins]
    in_specs.append(pl.BlockSpec((1, d), lambda i, b: (0, 0)))
    out_shapes, out_specs = [], []
    shape2d = (bsz * seq, d) if out_batch_major else (seq, bsz * d)
    ospec = bm_spec if out_batch_major else tm_spec
    if emit_sum:
        out_shapes.append(jax.ShapeDtypeStruct(shape2d, F32))
        out_specs.append(ospec())
    out_shapes.append(jax.ShapeDtypeStruct(shape2d, F32 if emit_norm_f32 else BF16))
    out_specs.append(ospec())
    outs = pl.pallas_call(
        functools.partial(_addnorm_kernel, n_in=len(ins), emit_sum=emit_sum,
                          emit_norm_f32=emit_norm_f32),
        grid=grid, in_specs=in_specs, out_specs=out_specs, out_shape=out_shapes,
        compiler_params=_params("parallel", "parallel"), name="addnorm",
    )(*ins, gain.reshape(1, d).astype(F32))
    return [o.reshape(bsz * seq, d) for o in outs]


def _mm_kernel(x_ref, w_ref, *rest, epilogue):
    o_ref = rest[-1]
    acc = jnp.dot(x_ref[...], w_ref[...], preferred_element_type=F32)
    if epilogue == "sigmoid":
        acc = jax.nn.sigmoid(acc)
    elif epilogue == "residual":
        acc = acc + rest[0][...]
    o_ref[...] = acc.astype(o_ref.dtype)


def _matmul(x, w, *, out_dtype, epilogue=None, residual=None, tm=1024, tn=1024):
    t, k = x.shape
    n = w.shape[1]
    tm, tn = min(tm, t), min(tn, n)
    in_specs = [pl.BlockSpec((tm, k), lambda i, j: (i, 0)),
                pl.BlockSpec((k, tn), lambda i, j: (0, j))]
    args = [x, w]
    if epilogue == "residual":
        in_specs.append(pl.BlockSpec((tm, tn), lambda i, j: (i, j)))
        args.append(residual)
    return pl.pallas_call(
        functools.partial(_mm_kernel, epilogue=epilogue),
        grid=(t // tm, n // tn), in_specs=in_specs,
        out_specs=pl.BlockSpec((tm, tn), lambda i, j: (i, j)),
        out_shape=jax.ShapeDtypeStruct((t, n), out_dtype),
        compiler_params=_params("parallel", "parallel"), name="matmul",
    )(*args)


def _glu_kernel(z_ref, wa_ref, wb_ref, o_ref):
    z = z_ref[...]
    a = jnp.dot(z, wa_ref[...], preferred_element_type=F32)
    b = jnp.dot(z, wb_ref[...], preferred_element_type=F32)
    o_ref[...] = (a * jax.nn.sigmoid(b)).astype(o_ref.dtype)


def _glu(z, w_glu, *, tm=1024, tn=512):
    t, k = z.shape
    n = w_glu.shape[1] // 2
    tm, tn = min(tm, t), min(tn, n)
    nb = n // tn
    return pl.pallas_call(
        _glu_kernel, grid=(t // tm, nb),
        in_specs=[pl.BlockSpec((tm, k), lambda i, j: (i, 0)),
                  pl.BlockSpec((k, tn), lambda i, j: (0, j)),
                  pl.BlockSpec((k, tn), lambda i, j: (0, j + nb))],
        out_specs=pl.BlockSpec((tm, tn), lambda i, j: (i, j)),
        out_shape=jax.ShapeDtypeStruct((t, n), BF16),
        compiler_params=_params("parallel", "parallel"), name="glu",
    )(z, w_glu, w_glu)


def _merge_kernel(a_ref, s_ref, ga_ref, gs_ref, wa_ref, ws_ref, o_ref):
    a = jnp.dot(a_ref[...], wa_ref[...], preferred_element_type=F32)
    s = jnp.dot(s_ref[...], ws_ref[...], preferred_element_type=F32)
    o_ref[...] = (ga_ref[...].astype(F32) * a + gs_ref[...].astype(F32) * s).astype(o_ref.dtype)


def _merge(attn, ssm, gates, w_ab, w_sb, *, tm=1024, tn=1024):
    t = attn.shape[0]
    n = w_ab.shape[1]
    tm, tn = min(tm, t), min(tn, n)
    nb = n // tn
    return pl.pallas_call(
        _merge_kernel, grid=(t // tm, nb),
        in_specs=[pl.BlockSpec((tm, attn.shape[1]), lambda i, j: (i, 0)),
                  pl.BlockSpec((tm, ssm.shape[1]), lambda i, j: (i, 0)),
                  pl.BlockSpec((tm, tn), lambda i, j: (i, j)),
                  pl.BlockSpec((tm, tn), lambda i, j: (i, j + nb)),
                  pl.BlockSpec((w_ab.shape[0], tn), lambda i, j: (0, j)),
                  pl.BlockSpec((w_sb.shape[0], tn), lambda i, j: (0, j))],
        out_specs=pl.BlockSpec((tm, tn), lambda i, j: (i, j)),
        out_shape=jax.ShapeDtypeStruct((t, n), BF16),
        compiler_params=_params("parallel", "parallel"), name="merge",
    )(attn, ssm, gates, gates, w_ab, w_sb)


def _alibi_slopes():
    return [float(v) for v in
            np.asarray(2.0 ** (-8.0 * np.arange(1, N_Q_HEADS + 1) / N_Q_HEADS), dtype=np.float32)]


def _attn_kernel(sink_ref, q0_ref, q1_ref, q2_ref, q3_ref, kc_ref, kp_ref, vc_ref, vp_ref, o_ref):
    n = pl.program_id(1)
    qi = lax.broadcasted_iota(jnp.int32, (BLOCK, 2 * BLOCK), 0)
    sj = lax.broadcasted_iota(jnp.int32, (BLOCK, 2 * BLOCK), 1)
    dist = qi - sj + BLOCK
    valid = (dist >= 0) & (dist < WINDOW) & ((n > 0) | (sj >= BLOCK))
    distf = dist.astype(F32)
    slopes = _alibi_slopes()
    scale = HEAD_DIM ** -0.5
    for k, q_ref in enumerate((q0_ref, q1_ref, q2_ref, q3_ref)):
        ksl = slice(k * HEAD_DIM, (k + 1) * HEAD_DIM)
        kw = jnp.concatenate([kp_ref[:, ksl], kc_ref[:, ksl]], axis=0)
        vw = jnp.concatenate([vp_ref[:, ksl], vc_ref[:, ksl]], axis=0)
        q4 = jnp.concatenate([q_ref[:, g * HEAD_DIM:(g + 1) * HEAD_DIM]
                              for g in range(Q_PER_KV)], axis=0)
        logits = lax.dot_general(q4, kw, (((1,), (1,)), ((), ())),
                                 preferred_element_type=F32) * scale
        ps, dens = [], []
        for g in range(Q_PER_KV):
            h = k * Q_PER_KV + g
            lg = logits[g * BLOCK:(g + 1) * BLOCK]
            lg = jnp.where(valid, lg - slopes[h] * distf, NEG_INF)
            sink = sink_ref[h]
            m = jnp.maximum(jnp.max(lg, axis=-1, keepdims=True), sink)
            p = jnp.exp(lg - m)
            dens.append(jnp.sum(p, axis=-1, keepdims=True) + jnp.exp(sink - m))
            ps.append(p.astype(BF16))
        pv = jnp.dot(jnp.concatenate(ps, axis=0), vw, preferred_element_type=F32)
        for g in range(Q_PER_KV):
            h = k * Q_PER_KV + g
            o_ref[:, h * HEAD_DIM:(h + 1) * HEAD_DIM] = (
                pv[g * BLOCK:(g + 1) * BLOCK] / dens[g]).astype(o_ref.dtype)


def _attention(qkv, sinks, *, bsz, seq):
    aw = N_Q_HEADS * HEAD_DIM
    kvw = N_KV_HEADS * HEAD_DIM
    assert aw == N_KV_HEADS * kvw
    width = aw + 2 * kvw
    nb = seq // BLOCK
    v2 = qkv.reshape(seq, bsz * width)
    units = width // kvw
    koff, voff = aw // kvw, aw // kvw + 1
    q_specs = [pl.BlockSpec((BLOCK, kvw), lambda b, n, g=g: (n, b * units + g))
               for g in range(N_KV_HEADS)]
    return pl.pallas_call(
        _attn_kernel, grid=(bsz, nb),
        in_specs=[pl.BlockSpec(memory_space=pltpu.SMEM)] + q_specs + [
            pl.BlockSpec((BLOCK, kvw), lambda b, n: (n, b * units + koff)),
            pl.BlockSpec((BLOCK, kvw), lambda b, n: (jnp.maximum(n - 1, 0), b * units + koff)),
            pl.BlockSpec((BLOCK, kvw), lambda b, n: (n, b * units + voff)),
            pl.BlockSpec((BLOCK, kvw), lambda b, n: (jnp.maximum(n - 1, 0), b * units + voff))],
        out_specs=pl.BlockSpec((BLOCK, aw), lambda b, n: (n, b)),
        out_shape=jax.ShapeDtypeStruct((seq, bsz * aw), BF16),
        compiler_params=_params("parallel", "parallel"), name="swa_attention",
    )(sinks.astype(F32), *([v2] * (N_KV_HEADS + 4))).reshape(bsz * seq, aw)


S5_SLICE_GROUPS = 16
S5_SLICE_IN = S5_SLICE_GROUPS * SSM_GROUP
S5_SLICE_STATE = S5_SLICE_GROUPS * SSM_STATE


def _s5_kernel(u_ref, bd_ref, lam_ref, cd_ref, d_ref, z_ref, bu_ref, st_ref, *, steps):
    ns = S5_SLICE_STATE

    @pl.when(pl.program_id(1) == 0)
    def _():
        st_ref[...] = jnp.zeros_like(st_ref)

    u = u_ref[...]
    bu_ref[...] = jnp.dot(u.astype(BF16), bd_ref[0], preferred_element_type=F32)
    lr = jnp.broadcast_to(lam_ref[0, 0:1, :], (SUBLANES, ns))
    li = jnp.broadcast_to(lam_ref[0, 1:2, :], (SUBLANES, ns))

    def step(t, carry):
        xr, xi = carry
        row = pl.multiple_of(t * SUBLANES, SUBLANES)
        nr = lr * xr - li * xi + bu_ref[pl.ds(row, SUBLANES), 0:ns]
        ni = lr * xi + li * xr + bu_ref[pl.ds(row, SUBLANES), ns:2 * ns]
        bu_ref[pl.ds(row, SUBLANES), 0:ns] = nr
        bu_ref[pl.ds(row, SUBLANES), ns:2 * ns] = ni
        return nr, ni

    xr, xi = lax.fori_loop(0, steps, step, (st_ref[:, 0:ns], st_ref[:, ns:2 * ns]), unroll=4)
    st_ref[:, 0:ns] = xr
    st_ref[:, ns:2 * ns] = xi
    y = jnp.dot(bu_ref[...].astype(BF16), cd_ref[0], preferred_element_type=F32)
    y = y + d_ref[0] * u
    z_ref[...] = jax.nn.gelu(y).astype(z_ref.dtype)


def _s5_prepare(a_re, a_im, log_step, b_re, b_im, c_re, c_im):
    g = a_re.shape[0]
    ns = g // S5_SLICE_GROUPS
    ar, ai = a_re.astype(F32), a_im.astype(F32)
    dt = jnp.exp(log_step.astype(F32))[:, None]
    mag = jnp.exp(ar * dt)
    lr = mag * jnp.cos(ai * dt)
    li = mag * jnp.sin(ai * dt)
    den = ar * ar + ai * ai
    nr = lr - 1.0
    zr = (nr * ar + li * ai) / den
    zi = (li * ar - nr * ai) / den
    br, bi = b_re.astype(F32), b_im.astype(F32)
    bbar_r = zr[..., None] * br - zi[..., None] * bi
    bbar_i = zr[..., None] * bi + zi[..., None] * br
    eye = jnp.eye(S5_SLICE_GROUPS, dtype=F32)

    def bdiag(m):
        m = m.reshape(ns, S5_SLICE_GROUPS, SSM_STATE, SSM_GROUP)
        return jnp.einsum("ngph,gk->nghkp", m, eye).reshape(ns, S5_SLICE_IN, S5_SLICE_STATE)

    def cdiag(m):
        m = m.reshape(ns, S5_SLICE_GROUPS, SSM_GROUP, SSM_STATE)
        return jnp.einsum("nghp,gk->nkpgh", m, eye).reshape(ns, S5_SLICE_STATE, S5_SLICE_IN)

    bd = jnp.concatenate([bdiag(bbar_r), bdiag(bbar_i)], axis=-1).astype(BF16)
    cd = jnp.concatenate([cdiag(c_re.astype(F32)), -cdiag(c_im.astype(F32))], axis=1).astype(BF16)
    lam = jnp.stack([lr.reshape(ns, S5_SLICE_STATE), li.reshape(ns, S5_SLICE_STATE)], axis=1)
    return bd, lam, cd


def _s5(u, prep, d_skip, *, bsz, seq, steps=128):
    assert bsz == SUBLANES
    bd, lam, cd = prep
    ns = bd.shape[0]
    steps = min(steps, seq)
    rows = steps * bsz
    return pl.pallas_call(
        functools.partial(_s5_kernel, steps=steps),
        grid=(ns, seq // steps),
        in_specs=[pl.BlockSpec((rows, S5_SLICE_IN), lambda j, c: (c, j)),
                  pl.BlockSpec((1, S5_SLICE_IN, 2 * S5_SLICE_STATE), lambda j, c: (j, 0, 0)),
                  pl.BlockSpec((1, 2, S5_SLICE_STATE), lambda j, c: (j, 0, 0)),
                  pl.BlockSpec((1, 2 * S5_SLICE_STATE, S5_SLICE_IN), lambda j, c: (j, 0, 0)),
                  pl.BlockSpec((1, 1, S5_SLICE_IN), lambda j, c: (j, 0, 0))],
        out_specs=pl.BlockSpec((rows, S5_SLICE_IN), lambda j, c: (c, j)),
        out_shape=jax.ShapeDtypeStruct(u.shape, BF16),
        scratch_shapes=[pltpu.VMEM((rows, 2 * S5_SLICE_STATE), F32),
                        pltpu.VMEM((SUBLANES, 2 * S5_SLICE_STATE), F32)],
        compiler_params=_params("parallel", "arbitrary"), name="s5_scan",
    )(u, bd, lam, cd, d_skip.astype(F32).reshape(ns, 1, S5_SLICE_IN))


ROUTE_KEEP = PEER_TOPK + 1
ROUTE_ROWS = 24


def _extract_sorted(s, count):
    iota = lax.broadcasted_iota(jnp.int32, s.shape, 0)
    big = s.shape[0]
    out = []
    for _ in range(count):
        m = jnp.max(s, axis=0, keepdims=True)
        out.append(m)
        first = jnp.min(jnp.where(s == m, iota, big), axis=0, keepdims=True)
        s = jnp.where(iota == first, -jnp.inf, s)
    return out


def _route_kernel(q_ref, keys_ref, theta_ref, coef_ref, s2_ref, e2_ref, a_scr, b_scr, *, tt):
    for h in range(PEER_HEADS):
        subs = []
        for c in range(2):
            qh = q_ref[:, (2 * h + c) * N_KEYS:(2 * h + c + 1) * N_KEYS].astype(BF16)
            subs.append(lax.dot_general(keys_ref[h, c], qh, (((1,), (1,)), ((), ())),
                                        preferred_element_type=F32))
        s1, s2 = subs
        a_scr[...] = jnp.full(a_scr.shape, -jnp.inf, F32)
        b_scr[...] = jnp.full(b_scr.shape, -jnp.inf, F32)
        a_rows = _extract_sorted(s1, ROUTE_KEEP)
        b_rows = _extract_sorted(s2, ROUTE_KEEP)
        for r in range(ROUTE_KEEP):
            a_scr[r:r + 1, :] = a_rows[r]
            b_scr[r:r + 1, :] = b_rows[r]
        cand = [a_rows[0] + b_scr[...]]
        for k in range(1, SUBLANES):
            cand.append(a_rows[k] + b_scr[0:SUBLANES, :])
        cand.append(a_scr[SUBLANES:ROUTE_ROWS, :] + b_rows[0])
        top = _extract_sorted(jnp.concatenate(cand, axis=0), ROUTE_KEEP)
        zsum = jnp.zeros((1, tt), F32)
        for r in range(PEER_TOPK):
            zsum = zsum + jnp.exp(top[r] - top[0])
        thr = 0.5 * (top[PEER_TOPK - 1] + top[PEER_TOPK])
        theta_ref[h] = thr - s1
        coef_ref[h] = jnp.exp(s1 - a_rows[0]) / zsum
        s2_ref[h] = s2
        e2_ref[h] = jnp.exp(s2 - b_rows[0])


def _route(qp, keys, *, tt=256):
    t = qp.shape[0]
    tt = min(tt, t)
    shp = jax.ShapeDtypeStruct((PEER_HEADS, N_KEYS, t), F32)
    ospec = pl.BlockSpec((PEER_HEADS, N_KEYS, tt), lambda i: (0, 0, i))
    return pl.pallas_call(
        functools.partial(_route_kernel, tt=tt), grid=(t // tt,),
        in_specs=[pl.BlockSpec((tt, qp.shape[1]), lambda i: (i, 0)),
                  pl.BlockSpec(keys.shape, lambda i: (0, 0, 0, 0))],
        out_specs=[ospec] * 4, out_shape=[shp] * 4,
        scratch_shapes=[pltpu.VMEM((ROUTE_ROWS, tt), F32), pltpu.VMEM((ROUTE_ROWS, tt), F32)],
        compiler_params=_params("parallel"), name="peer_route",
    )(qp, keys)


def _peer_kernel(x_ref, u_ref, v_ref, theta_ref, coef_ref, s2_ref, e2_ref, o_ref,
                 act_scr, hg_scr, *, eb, tm):
    e = pl.program_id(1)

    @pl.when(e == 0)
    def _():
        o_ref[...] = jnp.zeros_like(o_ref)

    ht = lax.dot_general(u_ref[...], x_ref[...], (((1,), (1,)), ((), ())),
                         preferred_element_type=F32)
    act_scr[...] = jax.nn.gelu(ht)
    for ii in range(eb // N_KEYS):
        rs = slice(ii * N_KEYS, (ii + 1) * N_KEYS)
        for lc in range(tm // LANES):
            ls = slice(lc * LANES, (lc + 1) * LANES)
            g = jnp.zeros((N_KEYS, LANES), F32)
            for h in range(PEER_HEADS):
                th = theta_ref[h, 0, ii:ii + 1, ls]
                cf = coef_ref[h, 0, ii:ii + 1, ls]
                g = g + jnp.where(s2_ref[h, :, ls] >= th, e2_ref[h, :, ls] * cf, 0.0)
            hg_scr[rs, ls] = (act_scr[rs, ls] * g).astype(BF16)
    o_ref[...] += lax.dot_general(hg_scr[...], v_ref[...], (((0,), (0,)), ((), ())),
                                  preferred_element_type=F32)


def _peer(xn, u_tab, v_tab, route, *, tm=512, eb=512):
    t, d = xn.shape
    n_exp = u_tab.shape[0]
    tm = min(tm, t)
    n_i = eb // N_KEYS
    theta, coef, s2, e2 = route
    theta = theta.reshape(PEER_HEADS, N_KEYS // n_i, n_i, t)
    coef = coef.reshape(PEER_HEADS, N_KEYS // n_i, n_i, t)
    ispec = pl.BlockSpec((PEER_HEADS, 1, n_i, tm), lambda i, e: (0, e, 0, i))
    rspec = pl.BlockSpec((PEER_HEADS, N_KEYS, tm), lambda i, e: (0, 0, i))
    route = (theta, coef, s2, e2)
    return pl.pallas_call(
        functools.partial(_peer_kernel, eb=eb, tm=tm), grid=(t // tm, n_exp // eb),
        in_specs=[pl.BlockSpec((tm, d), lambda i, e: (i, 0)),
                  pl.BlockSpec((eb, d), lambda i, e: (e, 0)),
                  pl.BlockSpec((eb, d), lambda i, e: (e, 0)),
                  ispec, ispec, rspec, rspec],
        out_specs=pl.BlockSpec((tm, d), lambda i, e: (i, 0)),
        out_shape=jax.ShapeDtypeStruct((t, d), F32),
        scratch_shapes=[pltpu.VMEM((eb, tm), F32), pltpu.VMEM((eb, tm), BF16)],
        compiler_params=_params("parallel", "arbitrary"), name="peer_experts",
    )(xn, u_tab, v_tab, *route)


def kernel(x, ln_mix, w_in, ssm_a_re, ssm_a_im, ssm_log_step, ssm_b_re, ssm_b_im, ssm_c_re,
           ssm_c_im, ssm_d, w_glu, attn_sinks, w_attn_branch, w_ssm_branch, w_out, ln_ffn,
           w_peer_q, peer_keys, peer_u, peer_v, ln_final):
    bsz, seq, d = x.shape
    depth = w_in.shape[0]
    aw = N_Q_HEADS * HEAD_DIM
    qkv_w = aw + 2 * N_KV_HEADS * HEAD_DIM
    ssm_w = ssm_d.shape[-1]
    kw = dict(bsz=bsz, seq=seq)

    h = x.reshape(bsz * seq, d)
    p = None
    for l in range(depth):
        if l == 0:
            xres, xn = _addnorm([h], ln_mix[l], in_batch_major=True, emit_sum=True, **kw)
        else:
            xres, xn = _addnorm([h, p], ln_mix[l], emit_sum=True, **kw)
        w_l = w_in[l].astype(BF16)
        qkv = _matmul(xn, w_l[:, :qkv_w], out_dtype=BF16)
        u = _matmul(xn, w_l[:, qkv_w:qkv_w + ssm_w], out_dtype=F32)
        gates = _matmul(xn, w_l[:, qkv_w + ssm_w:], out_dtype=BF16, epilogue="sigmoid")
        attn = _attention(qkv, attn_sinks[l], **kw)
        prep = _s5_prepare(ssm_a_re[l], ssm_a_im[l], ssm_log_step[l], ssm_b_re[l], ssm_b_im[l],
                           ssm_c_re[l], ssm_c_im[l])
        z = _s5(u, prep, ssm_d[l], **kw)
        ssm = _glu(z, w_glu[l].astype(BF16))
        merged = _merge(attn, ssm, gates, w_attn_branch[l].astype(BF16),
                        w_ssm_branch[l].astype(BF16))
        h = _matmul(merged, w_out[l].astype(BF16), out_dtype=F32, epilogue="residual",
                    residual=xres)
        (hn,) = _addnorm([h], ln_ffn[l], **kw)
        qp = _matmul(hn, w_peer_q[l].astype(BF16), out_dtype=F32)
        route = _route(qp, peer_keys[l].astype(BF16))
        p = _peer(hn, peer_u[l].astype(BF16), peer_v[l].astype(BF16), route)
    (out,) = _addnorm([h, p], ln_final, out_batch_major=True, emit_norm_f32=True, **kw)
    return out.reshape(bsz, seq, d)
```

```python
import functools

import numpy as np
import jax
import jax.numpy as jnp
from jax import lax
from jax.experimental import pallas as pl
from jax.experimental.pallas import tpu as pltpu

F32 = jnp.float32
BF16 = jnp.bfloat16

HEAD_DIM = 128
N_Q_HEADS = 16
N_KV_HEADS = 4
Q_PER_KV = 4
WINDOW = 128
BLOCK = 128
NEG_INF = -1e30
SSM_GROUP = 16
SSM_STATE = 64
PEER_HEADS = 8
N_KEYS = 128
PEER_TOPK = 16
NORM_EPS = 1e-6

V7X_VMEM_BYTES = 64 * 1024 * 1024
VMEM_LIMIT = V7X_VMEM_BYTES - 8 * 1024 * 1024
V7X_MXU_DIM = 256
SUBLANES = 8
LANES = 128


def _params(*sem, flags=None):
    return pltpu.CompilerParams(dimension_semantics=sem, vmem_limit_bytes=VMEM_LIMIT, flags=flags)


def _addnorm_kernel(*refs, has_delta, emit_sum, emit_norm_f32, emit_norm_t):
    x = refs[0][...]
    k = 1
    if has_delta:
        x = x + refs[k][...].T
        k += 1
    g_ref = refs[k]
    outs = refs[k + 1:]
    y = x * lax.rsqrt(jnp.mean(x * x, axis=-1, keepdims=True) + NORM_EPS)
    y = y * g_ref[...]
    k = 0
    if emit_sum:
        outs[k][...] = x
        k += 1
    if emit_norm_f32:
        outs[k][...] = y
    else:
        outs[k][...] = y.astype(BF16)
    if emit_norm_t:
        outs[k + 1][...] = y.T.astype(BF16)


def _addnorm(x, gain, *, delta_t=None, emit_sum=False, emit_norm_f32=False, emit_norm_t=False,
             rows=256):
    t, d = x.shape
    rows = min(rows, t)
    spec = pl.BlockSpec((rows, d), lambda i: (i, 0))
    spec_t = pl.BlockSpec((d, rows), lambda i: (0, i))
    args, in_specs = [x], [spec]
    if delta_t is not None:
        args.append(delta_t)
        in_specs.append(spec_t)
    args.append(gain.reshape(1, d).astype(F32))
    in_specs.append(pl.BlockSpec((1, d), lambda i: (0, 0)))
    out_shapes, out_specs = [], []
    if emit_sum:
        out_shapes.append(jax.ShapeDtypeStruct((t, d), F32))
        out_specs.append(spec)
    out_shapes.append(jax.ShapeDtypeStruct((t, d), F32 if emit_norm_f32 else BF16))
    out_specs.append(spec)
    if emit_norm_t:
        out_shapes.append(jax.ShapeDtypeStruct((d, t), BF16))
        out_specs.append(spec_t)
    return pl.pallas_call(
        functools.partial(_addnorm_kernel, has_delta=delta_t is not None, emit_sum=emit_sum,
                          emit_norm_f32=emit_norm_f32, emit_norm_t=emit_norm_t),
        grid=(t // rows,), in_specs=in_specs, out_specs=out_specs, out_shape=out_shapes,
        compiler_params=_params("parallel"), name="addnorm",
    )(*args)


def _mm_kernel(x_ref, w_ref, *rest, epilogue):
    o_ref = rest[-1]
    acc = jnp.dot(x_ref[...], w_ref[...], preferred_element_type=F32)
    if epilogue == "sigmoid":
        acc = jax.nn.sigmoid(acc)
    elif epilogue == "residual":
        acc = acc + rest[0][...]
    o_ref[...] = acc.astype(o_ref.dtype)


def _matmul(x, w, *, out_dtype, col0=0, n=None, epilogue=None, residual=None, tm=1024, tn=1024):
    t, k = x.shape
    n = w.shape[1] if n is None else n
    tm, tn = min(tm, t), min(tn, n)
    assert col0 % tn == 0 and n % tn == 0
    cb = col0 // tn
    in_specs = [pl.BlockSpec((tm, k), lambda i, j: (i, 0)),
                pl.BlockSpec((k, tn), lambda i, j: (0, j + cb))]
    args = [x, w]
    if epilogue == "residual":
        in_specs.append(pl.BlockSpec((tm, tn), lambda i, j: (i, j)))
        args.append(residual)
    return pl.pallas_call(
        functools.partial(_mm_kernel, epilogue=epilogue),
        grid=(t // tm, n // tn), in_specs=in_specs,
        out_specs=pl.BlockSpec((tm, tn), lambda i, j: (i, j)),
        out_shape=jax.ShapeDtypeStruct((t, n), out_dtype),
        compiler_params=_params("parallel", "parallel"), name="matmul",
    )(*args)


def _glu_kernel(z_ref, wa_ref, wb_ref, o_ref):
    z = z_ref[...]
    a = jnp.dot(z, wa_ref[...], preferred_element_type=F32)
    b = jnp.dot(z, wb_ref[...], preferred_element_type=F32)
    o_ref[...] = (a * jax.nn.sigmoid(b)).astype(o_ref.dtype)


def _glu(z, w_glu, *, tm=1024, tn=512):
    t, k = z.shape
    n = w_glu.shape[1] // 2
    tm, tn = min(tm, t), min(tn, n)
    nb = n // tn
    return pl.pallas_call(
        _glu_kernel, grid=(t // tm, nb),
        in_specs=[pl.BlockSpec((tm, k), lambda i, j: (i, 0)),
                  pl.BlockSpec((k, tn), lambda i, j: (0, j)),
                  pl.BlockSpec((k, tn), lambda i, j: (0, j + nb))],
        out_specs=pl.BlockSpec((tm, tn), lambda i, j: (i, j)),
        out_shape=jax.ShapeDtypeStruct((t, n), BF16),
        compiler_params=_params("parallel", "parallel"), name="glu",
    )(z, w_glu, w_glu)


def _merge_kernel(a_ref, s_ref, ga_ref, gs_ref, wa_ref, ws_ref, o_ref):
    a = jnp.dot(a_ref[...], wa_ref[...], preferred_element_type=F32)
    s = jnp.dot(s_ref[...], ws_ref[...], preferred_element_type=F32)
    o_ref[...] = (ga_ref[...].astype(F32) * a + gs_ref[...].astype(F32) * s).astype(o_ref.dtype)


def _merge(attn, ssm, gates, w_ab, w_sb, *, tm=1024, tn=1024):
    t = attn.shape[0]
    n = w_ab.shape[1]
    tm, tn = min(tm, t), min(tn, n)
    nb = n // tn
    return pl.pallas_call(
        _merge_kernel, grid=(t // tm, nb),
        in_specs=[pl.BlockSpec((tm, attn.shape[1]), lambda i, j: (i, 0)),
                  pl.BlockSpec((tm, ssm.shape[1]), lambda i, j: (i, 0)),
                  pl.BlockSpec((tm, tn), lambda i, j: (i, j)),
                  pl.BlockSpec((tm, tn), lambda i, j: (i, j + nb)),
                  pl.BlockSpec((w_ab.shape[0], tn), lambda i, j: (0, j)),
                  pl.BlockSpec((w_sb.shape[0], tn), lambda i, j: (0, j))],
        out_specs=pl.BlockSpec((tm, tn), lambda i, j: (i, j)),
        out_shape=jax.ShapeDtypeStruct((t, n), BF16),
        compiler_params=_params("parallel", "parallel"), name="merge",
    )(attn, ssm, gates, gates, w_ab, w_sb)


def _alibi_slopes():
    return [float(v) for v in
            np.asarray(2.0 ** (-8.0 * np.arange(1, N_Q_HEADS + 1) / N_Q_HEADS), dtype=np.float32)]


def _attn_kernel(sink_ref, q0_ref, q1_ref, q2_ref, q3_ref, kc_ref, kp_ref, vc_ref, vp_ref, o_ref):
    n = pl.program_id(1)
    qi = lax.broadcasted_iota(jnp.int32, (BLOCK, 2 * BLOCK), 0)
    sj = lax.broadcasted_iota(jnp.int32, (BLOCK, 2 * BLOCK), 1)
    dist = qi - sj + BLOCK
    valid = (dist >= 0) & (dist < WINDOW) & ((n > 0) | (sj >= BLOCK))
    distf = dist.astype(F32)
    slopes = _alibi_slopes()
    scale = HEAD_DIM ** -0.5
    for k, q_ref in enumerate((q0_ref, q1_ref, q2_ref, q3_ref)):
        ksl = slice(k * HEAD_DIM, (k + 1) * HEAD_DIM)
        kw = jnp.concatenate([kp_ref[:, ksl], kc_ref[:, ksl]], axis=0)
        vw = jnp.concatenate([vp_ref[:, ksl], vc_ref[:, ksl]], axis=0)
        q4 = jnp.concatenate([q_ref[:, g * HEAD_DIM:(g + 1) * HEAD_DIM]
                              for g in range(Q_PER_KV)], axis=0)
        logits = lax.dot_general(q4, kw, (((1,), (1,)), ((), ())),
                                 preferred_element_type=F32) * scale
        ps, dens = [], []
        for g in range(Q_PER_KV):
            h = k * Q_PER_KV + g
            lg = logits[g * BLOCK:(g + 1) * BLOCK]
            lg = jnp.where(valid, lg - slopes[h] * distf, NEG_INF)
            sink = sink_ref[h]
            m = jnp.maximum(jnp.max(lg, axis=-1, keepdims=True), sink)
            p = jnp.exp(lg - m)
            dens.append(jnp.sum(p, axis=-1, keepdims=True) + jnp.exp(sink - m))
            ps.append(p.astype(BF16))
        pv = jnp.dot(jnp.concatenate(ps, axis=0), vw, preferred_element_type=F32)
        for g in range(Q_PER_KV):
            h = k * Q_PER_KV + g
            o_ref[:, h * HEAD_DIM:(h + 1) * HEAD_DIM] = (
                pv[g * BLOCK:(g + 1) * BLOCK] / dens[g]).astype(o_ref.dtype)


def _attention(qkv, sinks, *, bsz, seq):
    aw = N_Q_HEADS * HEAD_DIM
    kvw = N_KV_HEADS * HEAD_DIM
    assert aw == N_KV_HEADS * kvw
    nb = seq // BLOCK
    koff, voff = aw // kvw, aw // kvw + 1
    q_specs = [pl.BlockSpec((BLOCK, kvw), lambda b, n, g=g: (b * nb + n, g))
               for g in range(N_KV_HEADS)]
    prev = lambda b, n: b * nb + jnp.maximum(n - 1, 0)
    return pl.pallas_call(
        _attn_kernel, grid=(bsz, nb),
        in_specs=[pl.BlockSpec(memory_space=pltpu.SMEM)] + q_specs + [
            pl.BlockSpec((BLOCK, kvw), lambda b, n: (b * nb + n, koff)),
            pl.BlockSpec((BLOCK, kvw), lambda b, n: (prev(b, n), koff)),
            pl.BlockSpec((BLOCK, kvw), lambda b, n: (b * nb + n, voff)),
            pl.BlockSpec((BLOCK, kvw), lambda b, n: (prev(b, n), voff))],
        out_specs=pl.BlockSpec((BLOCK, aw), lambda b, n: (b * nb + n, 0)),
        out_shape=jax.ShapeDtypeStruct((bsz * seq, aw), BF16),
        compiler_params=_params("parallel", "parallel"), name="swa_attention",
    )(sinks.astype(F32), *([qkv] * (N_KV_HEADS + 4)))


S5_SLICE_GROUPS = 16
S5_SLICE_IN = S5_SLICE_GROUPS * SSM_GROUP
S5_SLICE_STATE = S5_SLICE_GROUPS * SSM_STATE


def _s5_kernel(u_ref, bd_ref, lam_ref, cd_ref, d_ref, z_ref, bu_ref, st_ref, *, steps):
    ns = S5_SLICE_STATE

    @pl.when(pl.program_id(1) == 0)
    def _():
        st_ref[...] = jnp.zeros_like(st_ref)

    u = u_ref[...]
    bu_ref[...] = jnp.dot(u.astype(BF16), bd_ref[0], preferred_element_type=F32)
    lr = jnp.broadcast_to(lam_ref[0, 0:1, :], (SUBLANES, ns))
    li = jnp.broadcast_to(lam_ref[0, 1:2, :], (SUBLANES, ns))

    def step(t, carry):
        xr, xi = carry
        row = pl.multiple_of(t * SUBLANES, SUBLANES)
        nr = lr * xr - li * xi + bu_ref[pl.ds(row, SUBLANES), 0:ns]
        ni = lr * xi + li * xr + bu_ref[pl.ds(row, SUBLANES), ns:2 * ns]
        bu_ref[pl.ds(row, SUBLANES), 0:ns] = nr
        bu_ref[pl.ds(row, SUBLANES), ns:2 * ns] = ni
        return nr, ni

    xr, xi = lax.fori_loop(0, steps, step, (st_ref[:, 0:ns], st_ref[:, ns:2 * ns]), unroll=4)
    st_ref[:, 0:ns] = xr
    st_ref[:, ns:2 * ns] = xi
    y = jnp.dot(bu_ref[...].astype(BF16), cd_ref[0], preferred_element_type=F32)
    y = y + d_ref[0] * u
    z_ref[...] = jax.nn.gelu(y).astype(z_ref.dtype)


def _s5_prepare(a_re, a_im, log_step, b_re, b_im, c_re, c_im):
    g = a_re.shape[0]
    ns = g // S5_SLICE_GROUPS
    ar, ai = a_re.astype(F32), a_im.astype(F32)
    dt = jnp.exp(log_step.astype(F32))[:, None]
    mag = jnp.exp(ar * dt)
    lr = mag * jnp.cos(ai * dt)
    li = mag * jnp.sin(ai * dt)
    den = ar * ar + ai * ai
    nr = lr - 1.0
    zr = (nr * ar + li * ai) / den
    zi = (li * ar - nr * ai) / den
    br, bi = b_re.astype(F32), b_im.astype(F32)
    bbar_r = zr[..., None] * br - zi[..., None] * bi
    bbar_i = zr[..., None] * bi + zi[..., None] * br
    eye = jnp.eye(S5_SLICE_GROUPS, dtype=F32)

    def bdiag(m):
        m = m.reshape(ns, S5_SLICE_GROUPS, SSM_STATE, SSM_GROUP)
        return jnp.einsum("ngph,gk->nghkp", m, eye).reshape(ns, S5_SLICE_IN, S5_SLICE_STATE)

    def cdiag(m):
        m = m.reshape(ns, S5_SLICE_GROUPS, SSM_GROUP, SSM_STATE)
        return jnp.einsum("nghp,gk->nkpgh", m, eye).reshape(ns, S5_SLICE_STATE, S5_SLICE_IN)

    bd = jnp.concatenate([bdiag(bbar_r), bdiag(bbar_i)], axis=-1).astype(BF16)
    cd = jnp.concatenate([cdiag(c_re.astype(F32)), -cdiag(c_im.astype(F32))], axis=1).astype(BF16)
    lam = jnp.stack([lr.reshape(ns, S5_SLICE_STATE), li.reshape(ns, S5_SLICE_STATE)], axis=1)
    return bd, lam, cd


def _s5(u, prep, d_skip, *, bsz, seq, steps=128):
    assert bsz == SUBLANES
    bd, lam, cd = prep
    ns = bd.shape[0]
    steps = min(steps, seq)
    rows = steps * bsz
    return pl.pallas_call(
        functools.partial(_s5_kernel, steps=steps),
        grid=(ns, seq // steps),
        in_specs=[pl.BlockSpec((rows, S5_SLICE_IN), lambda j, c: (c, j)),
                  pl.BlockSpec((1, S5_SLICE_IN, 2 * S5_SLICE_STATE), lambda j, c: (j, 0, 0)),
                  pl.BlockSpec((1, 2, S5_SLICE_STATE), lambda j, c: (j, 0, 0)),
                  pl.BlockSpec((1, 2 * S5_SLICE_STATE, S5_SLICE_IN), lambda j, c: (j, 0, 0)),
                  pl.BlockSpec((1, 1, S5_SLICE_IN), lambda j, c: (j, 0, 0))],
        out_specs=pl.BlockSpec((rows, S5_SLICE_IN), lambda j, c: (c, j)),
        out_shape=jax.ShapeDtypeStruct(u.shape, BF16),
        scratch_shapes=[pltpu.VMEM((rows, 2 * S5_SLICE_STATE), F32),
                        pltpu.VMEM((SUBLANES, 2 * S5_SLICE_STATE), F32)],
        compiler_params=_params("parallel", "arbitrary"), name="s5_scan",
    )(u, bd, lam, cd, d_skip.astype(F32).reshape(ns, 1, S5_SLICE_IN))


ROUTE_KEEP = PEER_TOPK + 1
ROUTE_ROWS = 24


def _extract_sorted(s, count):
    iota = lax.broadcasted_iota(jnp.int32, s.shape, 0)
    big = s.shape[0]
    out = []
    for _ in range(count):
        m = jnp.max(s, axis=0, keepdims=True)
        out.append(m)
        first = jnp.min(jnp.where(s == m, iota, big), axis=0, keepdims=True)
        s = jnp.where(iota == first, -jnp.inf, s)
    return out


def _route_kernel(q_ref, keys_ref, theta_ref, coef_ref, s2_ref, e2_ref, a_scr, b_scr, *, tt):
    for h in range(PEER_HEADS):
        subs = []
        for c in range(2):
            qh = q_ref[:, (2 * h + c) * N_KEYS:(2 * h + c + 1) * N_KEYS].astype(BF16)
            subs.append(lax.dot_general(keys_ref[h, c], qh, (((1,), (1,)), ((), ())),
                                        preferred_element_type=F32))
        s1, s2 = subs
        a_scr[...] = jnp.full(a_scr.shape, -jnp.inf, F32)
        b_scr[...] = jnp.full(b_scr.shape, -jnp.inf, F32)
        a_rows = _extract_sorted(s1, ROUTE_KEEP)
        b_rows = _extract_sorted(s2, ROUTE_KEEP)
        for r in range(ROUTE_KEEP):
            a_scr[r:r + 1, :] = a_rows[r]
            b_scr[r:r + 1, :] = b_rows[r]
        cand = [a_rows[0] + b_scr[...]]
        for k in range(1, SUBLANES):
            cand.append(a_rows[k] + b_scr[0:SUBLANES, :])
        cand.append(a_scr[SUBLANES:ROUTE_ROWS, :] + b_rows[0])
        top = _extract_sorted(jnp.concatenate(cand, axis=0), ROUTE_KEEP)
        zsum = jnp.zeros((1, tt), F32)
        for r in range(PEER_TOPK):
            zsum = zsum + jnp.exp(top[r] - top[0])
        thr = 0.5 * (top[PEER_TOPK - 1] + top[PEER_TOPK])
        theta_ref[h] = thr - s1
        coef_ref[h] = jnp.exp(s1 - a_rows[0]) / zsum
        s2_ref[h] = s2
        e2_ref[h] = jnp.exp(s2 - b_rows[0])


def _route(qp, keys, *, tt=256):
    t = qp.shape[0]
    tt = min(tt, t)
    shp = jax.ShapeDtypeStruct((PEER_HEADS, N_KEYS, t), F32)
    ospec = pl.BlockSpec((PEER_HEADS, N_KEYS, tt), lambda i: (0, 0, i))
    return pl.pallas_call(
        functools.partial(_route_kernel, tt=tt), grid=(t // tt,),
        in_specs=[pl.BlockSpec((tt, qp.shape[1]), lambda i: (i, 0)),
                  pl.BlockSpec(keys.shape, lambda i: (0, 0, 0, 0))],
        out_specs=[ospec] * 4, out_shape=[shp] * 4,
        scratch_shapes=[pltpu.VMEM((ROUTE_ROWS, tt), F32), pltpu.VMEM((ROUTE_ROWS, tt), F32)],
        compiler_params=_params("parallel"), name="peer_route",
    )(qp, keys)


def _peer_kernel(xt_ref, u_ref, vt_ref, theta_ref, coef_ref, s2_ref, e2_ref, o_ref,
                 ht_scr, ht_next, hg_scr, *, eb, tm):
    e = pl.program_id(1)

    @pl.when(e == 0)
    def _():
        o_ref[...] = jnp.zeros_like(o_ref)
        ht_scr[...] = jnp.zeros_like(ht_scr)

    n_half = 2
    for c in range(n_half):
        ts = slice(c * (tm // n_half), (c + 1) * (tm // n_half))
        ht_next[:, ts] = jnp.dot(u_ref[...], xt_ref[:, ts], preferred_element_type=F32)
        es = slice(c * (eb // n_half), (c + 1) * (eb // n_half))
        for ii in range(es.start // N_KEYS, es.stop // N_KEYS):
            rs = slice(ii * N_KEYS, (ii + 1) * N_KEYS)
            for lc in range(tm // LANES):
                ls = slice(lc * LANES, (lc + 1) * LANES)
                g = jnp.zeros((N_KEYS, LANES), F32)
                for h in range(PEER_HEADS):
                    th = theta_ref[h, 0, ii:ii + 1, ls]
                    cf = coef_ref[h, 0, ii:ii + 1, ls]
                    g = g + jnp.where(s2_ref[h, :, ls] >= th, e2_ref[h, :, ls] * cf, 0.0)
                hg_scr[rs, ls] = (jax.nn.gelu(ht_scr[rs, ls]) * g).astype(BF16)
        o_ref[...] += jnp.dot(vt_ref[:, es], hg_scr[es, :], preferred_element_type=F32)
    ht_scr[...] = ht_next[...]


def _peer(xnt, u_tab, vt_tab, route, *, tm=512, eb=512):
    d, t = xnt.shape
    n_blk = u_tab.shape[0] // eb
    tm = min(tm, t)
    n_i = eb // N_KEYS
    theta, coef, s2, e2 = route
    theta = theta.reshape(PEER_HEADS, N_KEYS // n_i, n_i, t)
    coef = coef.reshape(PEER_HEADS, N_KEYS // n_i, n_i, t)
    cur = lambda e: jnp.minimum(e, n_blk - 1)
    prv = lambda e: jnp.maximum(e - 1, 0)
    ispec = pl.BlockSpec((PEER_HEADS, 1, n_i, tm), lambda i, e: (0, prv(e), 0, i))
    rspec = pl.BlockSpec((PEER_HEADS, N_KEYS, tm), lambda i, e: (0, 0, i))
    return pl.pallas_call(
        functools.partial(_peer_kernel, eb=eb, tm=tm), grid=(t // tm, n_blk + 1),
        in_specs=[pl.BlockSpec((d, tm), lambda i, e: (0, i)),
                  pl.BlockSpec((eb, d), lambda i, e: (cur(e), 0)),
                  pl.BlockSpec((d, eb), lambda i, e: (0, prv(e))),
                  ispec, ispec, rspec, rspec],
        out_specs=pl.BlockSpec((d, tm), lambda i, e: (0, i)),
        out_shape=jax.ShapeDtypeStruct((d, t), F32),
        scratch_shapes=[pltpu.VMEM((eb, tm), F32), pltpu.VMEM((eb, tm), F32),
                        pltpu.VMEM((eb, tm), BF16)],
        compiler_params=_params("parallel", "arbitrary"), name="peer_experts",
    )(xnt, u_tab, vt_tab, theta, coef, s2, e2)


def _to_time_major(a, bsz, seq):
    return a.reshape(bsz, seq, -1).transpose(1, 0, 2).reshape(seq * bsz, -1)


def _to_batch_major(a, bsz, seq):
    return a.reshape(seq, bsz, -1).transpose(1, 0, 2).reshape(bsz * seq, -1)


def kernel(x, ln_mix, w_in, ssm_a_re, ssm_a_im, ssm_log_step, ssm_b_re, ssm_b_im, ssm_c_re,
           ssm_c_im, ssm_d, w_glu, attn_sinks, w_attn_branch, w_ssm_branch, w_out, ln_ffn,
           w_peer_q, peer_keys, peer_u, peer_v, ln_final):
    bsz, seq, d = x.shape
    depth = w_in.shape[0]
    aw = N_Q_HEADS * HEAD_DIM
    qkv_w = aw + 2 * N_KV_HEADS * HEAD_DIM
    ssm_w = ssm_d.shape[-1]

    h = x.reshape(bsz * seq, d)
    pt = None
    for l in range(depth):
        if l == 0:
            xres = h
            (xn,) = _addnorm(h, ln_mix[l])
        else:
            xres, xn = _addnorm(h, ln_mix[l], delta_t=pt, emit_sum=True)
        w_l = w_in[l].astype(BF16)
        qkv = _matmul(xn, w_l, n=qkv_w, out_dtype=BF16)
        u = _matmul(xn, w_l, col0=qkv_w, n=ssm_w, out_dtype=F32)
        gates = _matmul(xn, w_l, col0=qkv_w + ssm_w, n=2 * d, out_dtype=BF16, epilogue="sigmoid")
        attn = _attention(qkv, attn_sinks[l], bsz=bsz, seq=seq)
        prep = _s5_prepare(ssm_a_re[l], ssm_a_im[l], ssm_log_step[l], ssm_b_re[l], ssm_b_im[l],
                           ssm_c_re[l], ssm_c_im[l])
        z = _s5(_to_time_major(u, bsz, seq), prep, ssm_d[l], bsz=bsz, seq=seq)
        ssm = _glu(_to_batch_major(z, bsz, seq), w_glu[l].astype(BF16))
        merged = _merge(attn, ssm, gates, w_attn_branch[l].astype(BF16),
                        w_ssm_branch[l].astype(BF16))
        h = _matmul(merged, w_out[l].astype(BF16), out_dtype=F32, epilogue="residual",
                    residual=xres)
        hn, hnt = _addnorm(h, ln_ffn[l], emit_norm_t=True)
        qp = _matmul(hn, w_peer_q[l].astype(BF16), out_dtype=F32)
        route = _route(qp, peer_keys[l].astype(BF16))
        pt = _peer(hnt, peer_u[l].astype(BF16), peer_v[l].T.astype(BF16), route)
    (out,) = _addnorm(h, ln_final, delta_t=pt, emit_norm_f32=True)
    return out.reshape(bsz, seq, d)
```

```python
import functools

import numpy as np
import jax
import jax.numpy as jnp
from jax import lax
from jax.experimental import pallas as pl
from jax.experimental.pallas import tpu as pltpu

F32 = jnp.float32
BF16 = jnp.bfloat16

HEAD_DIM = 128
N_Q_HEADS = 16
N_KV_HEADS = 4
Q_PER_KV = 4
WINDOW = 128
BLOCK = 128
NEG_INF = -1e30
SSM_GROUP = 16
SSM_STATE = 64
PEER_HEADS = 8
N_KEYS = 128
PEER_TOPK = 16
NORM_EPS = 1e-6

V7X_VMEM_BYTES = 64 * 1024 * 1024
VMEM_LIMIT = V7X_VMEM_BYTES - 8 * 1024 * 1024
V7X_MXU_DIM = 256
SUBLANES = 8
LANES = 128


def _params(*sem, flags=None):
    return pltpu.CompilerParams(dimension_semantics=sem, vmem_limit_bytes=VMEM_LIMIT, flags=flags)


def _addnorm_kernel(*refs, has_delta, emit_sum, emit_norm_f32, emit_norm_t):
    x = refs[0][...]
    k = 1
    if has_delta:
        x = x + refs[k][...].T
        k += 1
    g_ref = refs[k]
    outs = refs[k + 1:]
    y = x * lax.rsqrt(jnp.mean(x * x, axis=-1, keepdims=True) + NORM_EPS)
    y = y * g_ref[...]
    k = 0
    if emit_sum:
        outs[k][...] = x
        k += 1
    if emit_norm_f32:
        outs[k][...] = y
    else:
        outs[k][...] = y.astype(BF16)
    if emit_norm_t:
        outs[k + 1][...] = y.T.astype(BF16)


def _addnorm(x, gain, *, delta_t=None, emit_sum=False, emit_norm_f32=False, emit_norm_t=False,
             rows=256):
    t, d = x.shape
    rows = min(rows, t)
    spec = pl.BlockSpec((rows, d), lambda i: (i, 0))
    spec_t = pl.BlockSpec((d, rows), lambda i: (0, i))
    args, in_specs = [x], [spec]
    if delta_t is not None:
        args.append(delta_t)
        in_specs.append(spec_t)
    args.append(gain.reshape(1, d).astype(F32))
    in_specs.append(pl.BlockSpec((1, d), lambda i: (0, 0)))
    out_shapes, out_specs = [], []
    if emit_sum:
        out_shapes.append(jax.ShapeDtypeStruct((t, d), F32))
        out_specs.append(spec)
    out_shapes.append(jax.ShapeDtypeStruct((t, d), F32 if emit_norm_f32 else BF16))
    out_specs.append(spec)
    if emit_norm_t:
        out_shapes.append(jax.ShapeDtypeStruct((d, t), BF16))
        out_specs.append(spec_t)
    return pl.pallas_call(
        functools.partial(_addnorm_kernel, has_delta=delta_t is not None, emit_sum=emit_sum,
                          emit_norm_f32=emit_norm_f32, emit_norm_t=emit_norm_t),
        grid=(t // rows,), in_specs=in_specs, out_specs=out_specs, out_shape=out_shapes,
        compiler_params=_params("parallel"), name="addnorm",
    )(*args)


def _mm_kernel(x_ref, w_ref, *rest, epilogue):
    o_ref = rest[-1]
    acc = jnp.dot(x_ref[...], w_ref[...], preferred_element_type=F32)
    if epilogue == "sigmoid":
        acc = jax.nn.sigmoid(acc)
    elif epilogue == "residual":
        acc = acc + rest[0][...]
    o_ref[...] = acc.astype(o_ref.dtype)


def _matmul(x, w, *, out_dtype, col0=0, n=None, epilogue=None, residual=None, tm=1024, tn=1024):
    t, k = x.shape
    n = w.shape[1] if n is None else n
    tm, tn = min(tm, t), min(tn, n)
    assert col0 % tn == 0 and n % tn == 0
    cb = col0 // tn
    in_specs = [pl.BlockSpec((tm, k), lambda i, j: (i, 0)),
                pl.BlockSpec((k, tn), lambda i, j: (0, j + cb))]
    args = [x, w]
    if epilogue == "residual":
        in_specs.append(pl.BlockSpec((tm, tn), lambda i, j: (i, j)))
        args.append(residual)
    return pl.pallas_call(
        functools.partial(_mm_kernel, epilogue=epilogue),
        grid=(t // tm, n // tn), in_specs=in_specs,
        out_specs=pl.BlockSpec((tm, tn), lambda i, j: (i, j)),
        out_shape=jax.ShapeDtypeStruct((t, n), out_dtype),
        compiler_params=_params("parallel", "parallel"), name="matmul",
    )(*args)


def _glu_kernel(z_ref, wa_ref, wb_ref, o_ref):
    z = z_ref[...]
    a = jnp.dot(z, wa_ref[...], preferred_element_type=F32)
    b = jnp.dot(z, wb_ref[...], preferred_element_type=F32)
    o_ref[...] = (a * jax.nn.sigmoid(b)).astype(o_ref.dtype)


def _glu(z, w_glu, *, tm=1024, tn=512):
    t, k = z.shape
    n = w_glu.shape[1] // 2
    tm, tn = min(tm, t), min(tn, n)
    nb = n // tn
    return pl.pallas_call(
        _glu_kernel, grid=(t // tm, nb),
        in_specs=[pl.BlockSpec((tm, k), lambda i, j: (i, 0)),
                  pl.BlockSpec((k, tn), lambda i, j: (0, j)),
                  pl.BlockSpec((k, tn), lambda i, j: (0, j + nb))],
        out_specs=pl.BlockSpec((tm, tn), lambda i, j: (i, j)),
        out_shape=jax.ShapeDtypeStruct((t, n), BF16),
        compiler_params=_params("parallel", "parallel"), name="glu",
    )(z, w_glu, w_glu)


def _merge_kernel(a_ref, s_ref, ga_ref, gs_ref, wa_ref, ws_ref, o_ref):
    a = jnp.dot(a_ref[...], wa_ref[...], preferred_element_type=F32)
    s = jnp.dot(s_ref[...], ws_ref[...], preferred_element_type=F32)
    o_ref[...] = (ga_ref[...].astype(F32) * a + gs_ref[...].astype(F32) * s).astype(o_ref.dtype)


def _merge(attn, ssm, gates, w_ab, w_sb, *, tm=1024, tn=1024):
    t = attn.shape[0]
    n = w_ab.shape[1]
    tm, tn = min(tm, t), min(tn, n)
    nb = n // tn
    return pl.pallas_call(
        _merge_kernel, grid=(t // tm, nb),
        in_specs=[pl.BlockSpec((tm, attn.shape[1]), lambda i, j: (i, 0)),
                  pl.BlockSpec((tm, ssm.shape[1]), lambda i, j: (i, 0)),
                  pl.BlockSpec((tm, tn), lambda i, j: (i, j)),
                  pl.BlockSpec((tm, tn), lambda i, j: (i, j + nb)),
                  pl.BlockSpec((w_ab.shape[0], tn), lambda i, j: (0, j)),
                  pl.BlockSpec((w_sb.shape[0], tn), lambda i, j: (0, j))],
        out_specs=pl.BlockSpec((tm, tn), lambda i, j: (i, j)),
        out_shape=jax.ShapeDtypeStruct((t, n), BF16),
        compiler_params=_params("parallel", "parallel"), name="merge",
    )(attn, ssm, gates, gates, w_ab, w_sb)


def _alibi_slopes():
    return [float(v) for v in
            np.asarray(2.0 ** (-8.0 * np.arange(1, N_Q_HEADS + 1) / N_Q_HEADS), dtype=np.float32)]


def _attn_kernel(sink_ref, q0_ref, q1_ref, q2_ref, q3_ref, kc_ref, kp_ref, vc_ref, vp_ref, o_ref):
    n = pl.program_id(1)
    qi = lax.broadcasted_iota(jnp.int32, (BLOCK, 2 * BLOCK), 0)
    sj = lax.broadcasted_iota(jnp.int32, (BLOCK, 2 * BLOCK), 1)
    dist = qi - sj + BLOCK
    valid = (dist >= 0) & (dist < WINDOW) & ((n > 0) | (sj >= BLOCK))
    distf = dist.astype(F32)
    slopes = _alibi_slopes()
    scale = HEAD_DIM ** -0.5
    for k, q_ref in enumerate((q0_ref, q1_ref, q2_ref, q3_ref)):
        ksl = slice(k * HEAD_DIM, (k + 1) * HEAD_DIM)
        kw = jnp.concatenate([kp_ref[:, ksl], kc_ref[:, ksl]], axis=0)
        vw = jnp.concatenate([vp_ref[:, ksl], vc_ref[:, ksl]], axis=0)
        q4 = jnp.concatenate([q_ref[:, g * HEAD_DIM:(g + 1) * HEAD_DIM]
                              for g in range(Q_PER_KV)], axis=0)
        logits = lax.dot_general(q4, kw, (((1,), (1,)), ((), ())),
                                 preferred_element_type=F32) * scale
        ps, dens = [], []
        for g in range(Q_PER_KV):
            h = k * Q_PER_KV + g
            lg = logits[g * BLOCK:(g + 1) * BLOCK]
            lg = jnp.where(valid, lg - slopes[h] * distf, NEG_INF)
            sink = sink_ref[h]
            m = jnp.maximum(jnp.max(lg, axis=-1, keepdims=True), sink)
            p = jnp.exp(lg - m)
            dens.append(jnp.sum(p, axis=-1, keepdims=True) + jnp.exp(sink - m))
            ps.append(p.astype(BF16))
        pv = jnp.dot(jnp.concatenate(ps, axis=0), vw, preferred_element_type=F32)
        for g in range(Q_PER_KV):
            h = k * Q_PER_KV + g
            o_ref[:, h * HEAD_DIM:(h + 1) * HEAD_DIM] = (
                pv[g * BLOCK:(g + 1) * BLOCK] / dens[g]).astype(o_ref.dtype)


def _attention(qkv, sinks, *, bsz, seq):
    aw = N_Q_HEADS * HEAD_DIM
    kvw = N_KV_HEADS * HEAD_DIM
    assert aw == N_KV_HEADS * kvw
    nb = seq // BLOCK
    koff, voff = aw // kvw, aw // kvw + 1
    q_specs = [pl.BlockSpec((BLOCK, kvw), lambda b, n, g=g: (b * nb + n, g))
               for g in range(N_KV_HEADS)]
    prev = lambda b, n: b * nb + jnp.maximum(n - 1, 0)
    return pl.pallas_call(
        _attn_kernel, grid=(bsz, nb),
        in_specs=[pl.BlockSpec(memory_space=pltpu.SMEM)] + q_specs + [
            pl.BlockSpec((BLOCK, kvw), lambda b, n: (b * nb + n, koff)),
            pl.BlockSpec((BLOCK, kvw), lambda b, n: (prev(b, n), koff)),
            pl.BlockSpec((BLOCK, kvw), lambda b, n: (b * nb + n, voff)),
            pl.BlockSpec((BLOCK, kvw), lambda b, n: (prev(b, n), voff))],
        out_specs=pl.BlockSpec((BLOCK, aw), lambda b, n: (b * nb + n, 0)),
        out_shape=jax.ShapeDtypeStruct((bsz * seq, aw), BF16),
        compiler_params=_params("parallel", "parallel"), name="swa_attention",
    )(sinks.astype(F32), *([qkv] * (N_KV_HEADS + 4)))


S5_SLICE_GROUPS = 16
S5_SLICE_IN = S5_SLICE_GROUPS * SSM_GROUP
S5_SLICE_STATE = S5_SLICE_GROUPS * SSM_STATE


def _s5_kernel(u_ref, bd_ref, lam_ref, cd_ref, d_ref, z_ref, bu_ref, st_ref, *, steps):
    ns = S5_SLICE_STATE

    @pl.when(pl.program_id(1) == 0)
    def _():
        st_ref[...] = jnp.zeros_like(st_ref)

    u = u_ref[...]
    bu_ref[...] = jnp.dot(u.astype(BF16), bd_ref[0], preferred_element_type=F32)
    lr = jnp.broadcast_to(lam_ref[0, 0:1, :], (SUBLANES, ns))
    li = jnp.broadcast_to(lam_ref[0, 1:2, :], (SUBLANES, ns))

    def step(t, carry):
        xr, xi = carry
        row = pl.multiple_of(t * SUBLANES, SUBLANES)
        nr = lr * xr - li * xi + bu_ref[pl.ds(row, SUBLANES), 0:ns]
        ni = lr * xi + li * xr + bu_ref[pl.ds(row, SUBLANES), ns:2 * ns]
        bu_ref[pl.ds(row, SUBLANES), 0:ns] = nr
        bu_ref[pl.ds(row, SUBLANES), ns:2 * ns] = ni
        return nr, ni

    xr, xi = lax.fori_loop(0, steps, step, (st_ref[:, 0:ns], st_ref[:, ns:2 * ns]), unroll=4)
    st_ref[:, 0:ns] = xr
    st_ref[:, ns:2 * ns] = xi
    y = jnp.dot(bu_ref[...].astype(BF16), cd_ref[0], preferred_element_type=F32)
    y = y + d_ref[0] * u
    z_ref[...] = jax.nn.gelu(y).astype(z_ref.dtype)


def _s5_prepare(a_re, a_im, log_step, b_re, b_im, c_re, c_im):
    g = a_re.shape[0]
    ns = g // S5_SLICE_GROUPS
    ar, ai = a_re.astype(F32), a_im.astype(F32)
    dt = jnp.exp(log_step.astype(F32))[:, None]
    mag = jnp.exp(ar * dt)
    lr = mag * jnp.cos(ai * dt)
    li = mag * jnp.sin(ai * dt)
    den = ar * ar + ai * ai
    nr = lr - 1.0
    zr = (nr * ar + li * ai) / den
    zi = (li * ar - nr * ai) / den
    br, bi = b_re.astype(F32), b_im.astype(F32)
    bbar_r = zr[..., None] * br - zi[..., None] * bi
    bbar_i = zr[..., None] * bi + zi[..., None] * br
    eye = jnp.eye(S5_SLICE_GROUPS, dtype=F32)

    def bdiag(m):
        m = m.reshape(ns, S5_SLICE_GROUPS, SSM_STATE, SSM_GROUP)
        return jnp.einsum("ngph,gk->nghkp", m, eye).reshape(ns, S5_SLICE_IN, S5_SLICE_STATE)

    def cdiag(m):
        m = m.reshape(ns, S5_SLICE_GROUPS, SSM_GROUP, SSM_STATE)
        return jnp.einsum("nghp,gk->nkpgh", m, eye).reshape(ns, S5_SLICE_STATE, S5_SLICE_IN)

    bd = jnp.concatenate([bdiag(bbar_r), bdiag(bbar_i)], axis=-1).astype(BF16)
    cd = jnp.concatenate([cdiag(c_re.astype(F32)), -cdiag(c_im.astype(F32))], axis=1).astype(BF16)
    lam = jnp.stack([lr.reshape(ns, S5_SLICE_STATE), li.reshape(ns, S5_SLICE_STATE)], axis=1)
    return bd, lam, cd


def _s5(u, prep, d_skip, *, bsz, seq, steps=128):
    assert bsz == SUBLANES
    bd, lam, cd = prep
    ns = bd.shape[0]
    steps = min(steps, seq)
    rows = steps * bsz
    return pl.pallas_call(
        functools.partial(_s5_kernel, steps=steps),
        grid=(ns, seq // steps),
        in_specs=[pl.BlockSpec((rows, S5_SLICE_IN), lambda j, c: (c, j)),
                  pl.BlockSpec((1, S5_SLICE_IN, 2 * S5_SLICE_STATE), lambda j, c: (j, 0, 0)),
                  pl.BlockSpec((1, 2, S5_SLICE_STATE), lambda j, c: (j, 0, 0)),
                  pl.BlockSpec((1, 2 * S5_SLICE_STATE, S5_SLICE_IN), lambda j, c: (j, 0, 0)),
                  pl.BlockSpec((1, 1, S5_SLICE_IN), lambda j, c: (j, 0, 0))],
        out_specs=pl.BlockSpec((rows, S5_SLICE_IN), lambda j, c: (c, j)),
        out_shape=jax.ShapeDtypeStruct(u.shape, BF16),
        scratch_shapes=[pltpu.VMEM((rows, 2 * S5_SLICE_STATE), F32),
                        pltpu.VMEM((SUBLANES, 2 * S5_SLICE_STATE), F32)],
        compiler_params=_params("parallel", "arbitrary"), name="s5_scan",
    )(u, bd, lam, cd, d_skip.astype(F32).reshape(ns, 1, S5_SLICE_IN))


ROUTE_KEEP = PEER_TOPK + 1
ROUTE_ROWS = 24


def _sorting_network(n):
    pairs = []
    p = 1
    while p < n:
        k = p
        while k >= 1:
            for j in range(k % p, n - k, 2 * k):
                for i in range(min(k, n - j - k)):
                    if (i + j) // (2 * p) == (i + j + k) // (2 * p):
                        pairs.append((i + j, i + j + k))
            k //= 2
        p *= 2
    return pairs


def _extract_sorted(s, count):
    n = s.shape[0] // SUBLANES
    v = [s[r * SUBLANES:(r + 1) * SUBLANES] for r in range(n)]
    wires = 1 << (n - 1).bit_length()
    v += [None] * (wires - n)
    for i, j in _sorting_network(wires):
        if v[j] is None:
            continue
        if v[i] is None:
            v[i], v[j] = v[j], None
        else:
            v[i], v[j] = jnp.maximum(v[i], v[j]), jnp.minimum(v[i], v[j])
    v = v[:n]
    sub = lax.broadcasted_iota(jnp.int32, v[0].shape, 0).astype(F32)
    out = []
    for k in range(count):
        m = jnp.max(v[0], axis=0, keepdims=True)
        out.append(m)
        first = jnp.min(jnp.where(v[0] == m, sub, float(SUBLANES)), axis=0, keepdims=True)
        pop = sub == first
        for r in range(min(n, count - k - 1)):
            nxt = v[r + 1] if r + 1 < n else -jnp.inf
            v[r] = jnp.where(pop, nxt, v[r])
    return out


def _route_kernel(q_ref, keys_ref, theta_ref, coef_ref, s2_ref, e2_ref, a_scr, b_scr, *, tt):
    for h in range(PEER_HEADS):
        subs = []
        for c in range(2):
            qh = q_ref[:, (2 * h + c) * N_KEYS:(2 * h + c + 1) * N_KEYS].astype(BF16)
            subs.append(lax.dot_general(keys_ref[h, c], qh, (((1,), (1,)), ((), ())),
                                        preferred_element_type=F32))
        s1, s2 = subs
        a_scr[...] = jnp.full(a_scr.shape, -jnp.inf, F32)
        b_scr[...] = jnp.full(b_scr.shape, -jnp.inf, F32)
        a_rows = _extract_sorted(s1, ROUTE_KEEP)
        b_rows = _extract_sorted(s2, ROUTE_KEEP)
        for r in range(ROUTE_KEEP):
            a_scr[r:r + 1, :] = a_rows[r]
            b_scr[r:r + 1, :] = b_rows[r]
        cand = [a_rows[0] + b_scr[...]]
        for k in range(1, SUBLANES):
            cand.append(a_rows[k] + b_scr[0:SUBLANES, :])
        cand.append(a_scr[SUBLANES:ROUTE_ROWS, :] + b_rows[0])
        top = _extract_sorted(jnp.concatenate(cand, axis=0), ROUTE_KEEP)
        zsum = jnp.zeros((1, tt), F32)
        for r in range(PEER_TOPK):
            zsum = zsum + jnp.exp(top[r] - top[0])
        thr = 0.5 * (top[PEER_TOPK - 1] + top[PEER_TOPK])
        theta_ref[h] = thr - s1
        coef_ref[h] = jnp.exp(s1 - a_rows[0]) / zsum
        s2_ref[h] = s2
        e2_ref[h] = jnp.exp(s2 - b_rows[0])


def _route(qp, keys, *, tt=256):
    t = qp.shape[0]
    tt = min(tt, t)
    shp = jax.ShapeDtypeStruct((PEER_HEADS, N_KEYS, t), F32)
    ospec = pl.BlockSpec((PEER_HEADS, N_KEYS, tt), lambda i: (0, 0, i))
    return pl.pallas_call(
        functools.partial(_route_kernel, tt=tt), grid=(t // tt,),
        in_specs=[pl.BlockSpec((tt, qp.shape[1]), lambda i: (i, 0)),
                  pl.BlockSpec(keys.shape, lambda i: (0, 0, 0, 0))],
        out_specs=[ospec] * 4, out_shape=[shp] * 4,
        scratch_shapes=[pltpu.VMEM((ROUTE_ROWS, tt), F32), pltpu.VMEM((ROUTE_ROWS, tt), F32)],
        compiler_params=_params("parallel"), name="peer_route",
    )(qp, keys)


def _peer_kernel(xt_ref, u_ref, vt_ref, theta_ref, coef_ref, s2_ref, e2_ref, o_ref,
                 ht_scr, ht_next, hg_scr, *, eb, tm):
    e = pl.program_id(1)

    @pl.when(e == 0)
    def _():
        o_ref[...] = jnp.zeros_like(o_ref)
        ht_scr[...] = jnp.zeros_like(ht_scr)

    n_half = 2
    for c in range(n_half):
        ts = slice(c * (tm // n_half), (c + 1) * (tm // n_half))
        ht_next[:, ts] = jnp.dot(u_ref[...], xt_ref[:, ts], preferred_element_type=F32)
        es = slice(c * (eb // n_half), (c + 1) * (eb // n_half))
        for ii in range(es.start // N_KEYS, es.stop // N_KEYS):
            rs = slice(ii * N_KEYS, (ii + 1) * N_KEYS)
            for lc in range(tm // LANES):
                ls = slice(lc * LANES, (lc + 1) * LANES)
                g = jnp.zeros((N_KEYS, LANES), F32)
                for h in range(PEER_HEADS):
                    th = theta_ref[h, 0, ii:ii + 1, ls]
                    cf = coef_ref[h, 0, ii:ii + 1, ls]
                    g = g + jnp.where(s2_ref[h, :, ls] >= th, e2_ref[h, :, ls] * cf, 0.0)
                hg_scr[rs, ls] = (jax.nn.gelu(ht_scr[rs, ls]) * g).astype(BF16)
        o_ref[...] += jnp.dot(vt_ref[:, es], hg_scr[es, :], preferred_element_type=F32)
    ht_scr[...] = ht_next[...]


def _peer(xnt, u_tab, vt_tab, route, *, tm=512, eb=512):
    d, t = xnt.shape
    n_blk = u_tab.shape[0] // eb
    tm = min(tm, t)
    n_i = eb // N_KEYS
    theta, coef, s2, e2 = route
    theta = theta.reshape(PEER_HEADS, N_KEYS // n_i, n_i, t)
    coef = coef.reshape(PEER_HEADS, N_KEYS // n_i, n_i, t)
    cur = lambda e: jnp.minimum(e, n_blk - 1)
    prv = lambda e: jnp.maximum(e - 1, 0)
    ispec = pl.BlockSpec((PEER_HEADS, 1, n_i, tm), lambda i, e: (0, prv(e), 0, i))
    rspec = pl.BlockSpec((PEER_HEADS, N_KEYS, tm), lambda i, e: (0, 0, i))
    return pl.pallas_call(
        functools.partial(_peer_kernel, eb=eb, tm=tm), grid=(t // tm, n_blk + 1),
        in_specs=[pl.BlockSpec((d, tm), lambda i, e: (0, i)),
                  pl.BlockSpec((eb, d), lambda i, e: (cur(e), 0)),
                  pl.BlockSpec((d, eb), lambda i, e: (0, prv(e))),
                  ispec, ispec, rspec, rspec],
        out_specs=pl.BlockSpec((d, tm), lambda i, e: (0, i)),
        out_shape=jax.ShapeDtypeStruct((d, t), F32),
        scratch_shapes=[pltpu.VMEM((eb, tm), F32), pltpu.VMEM((eb, tm), F32),
                        pltpu.VMEM((eb, tm), BF16)],
        compiler_params=_params("parallel", "arbitrary"), name="peer_experts",
    )(xnt, u_tab, vt_tab, theta, coef, s2, e2)


def _to_time_major(a, bsz, seq):
    return a.reshape(bsz, seq, -1).transpose(1, 0, 2).reshape(seq * bsz, -1)


def _to_batch_major(a, bsz, seq):
    return a.reshape(seq, bsz, -1).transpose(1, 0, 2).reshape(bsz * seq, -1)


def kernel(x, ln_mix, w_in, ssm_a_re, ssm_a_im, ssm_log_step, ssm_b_re, ssm_b_im, ssm_c_re,
           ssm_c_im, ssm_d, w_glu, attn_sinks, w_attn_branch, w_ssm_branch, w_out, ln_ffn,
           w_peer_q, peer_keys, peer_u, peer_v, ln_final):
    bsz, seq, d = x.shape
    depth = w_in.shape[0]
    aw = N_Q_HEADS * HEAD_DIM
    qkv_w = aw + 2 * N_KV_HEADS * HEAD_DIM
    ssm_w = ssm_d.shape[-1]

    h = x.reshape(bsz * seq, d)
    pt = None
    for l in range(depth):
        if l == 0:
            xres = h
            (xn,) = _addnorm(h, ln_mix[l])
        else:
            xres, xn = _addnorm(h, ln_mix[l], delta_t=pt, emit_sum=True)
        w_l = w_in[l].astype(BF16)
        qkv = _matmul(xn, w_l, n=qkv_w, out_dtype=BF16)
        u = _matmul(xn, w_l, col0=qkv_w, n=ssm_w, out_dtype=F32)
        gates = _matmul(xn, w_l, col0=qkv_w + ssm_w, n=2 * d, out_dtype=BF16, epilogue="sigmoid")
        attn = _attention(qkv, attn_sinks[l], bsz=bsz, seq=seq)
        prep = _s5_prepare(ssm_a_re[l], ssm_a_im[l], ssm_log_step[l], ssm_b_re[l], ssm_b_im[l],
                           ssm_c_re[l], ssm_c_im[l])
        z = _s5(_to_time_major(u, bsz, seq), prep, ssm_d[l], bsz=bsz, seq=seq)
        ssm = _glu(_to_batch_major(z, bsz, seq), w_glu[l].astype(BF16))
        merged = _merge(attn, ssm, gates, w_attn_branch[l].astype(BF16),
                        w_ssm_branch[l].astype(BF16))
        h = _matmul(merged, w_out[l].astype(BF16), out_dtype=F32, epilogue="residual",
                    residual=xres)
        hn, hnt = _addnorm(h, ln_ffn[l], emit_norm_t=True)
        qp = _matmul(hn, w_peer_q[l].astype(BF16), out_dtype=F32)
        route = _route(qp, peer_keys[l].astype(BF16))
        pt = _peer(hnt, peer_u[l].astype(BF16), peer_v[l].T.astype(BF16), route)
    (out,) = _addnorm(h, ln_final, delta_t=pt, emit_norm_f32=True)
    return out.reshape(bsz, seq, d)
```

```python
import functools

import numpy as np
import jax
import jax.numpy as jnp
from jax import lax
from jax.experimental import pallas as pl
from jax.experimental.pallas import tpu as pltpu

F32 = jnp.float32
BF16 = jnp.bfloat16

HEAD_DIM = 128
N_Q_HEADS = 16
N_KV_HEADS = 4
Q_PER_KV = 4
WINDOW = 128
BLOCK = 128
NEG_INF = -1e30
SSM_GROUP = 16
SSM_STATE = 64
PEER_HEADS = 8
N_KEYS = 128
PEER_TOPK = 16
NORM_EPS = 1e-6

V7X_VMEM_BYTES = 64 * 1024 * 1024
VMEM_LIMIT = V7X_VMEM_BYTES - 8 * 1024 * 1024
V7X_MXU_DIM = 256
SUBLANES = 8
LANES = 128


def _params(*sem, flags=None):
    return pltpu.CompilerParams(dimension_semantics=sem, vmem_limit_bytes=VMEM_LIMIT, flags=flags)


CAST_BLOCK_ELEMS = 2 * 1024 * 1024


def _cast_kernel(w_ref, o_ref, *, transpose):
    w = w_ref[...]
    o_ref[...] = (w.T if transpose else w).astype(o_ref.dtype)


def _cast_bf16(w, l, *, transpose=False):
    _, r, c = w.shape
    rows = r
    while rows * c > CAST_BLOCK_ELEMS and rows % 2 == 0 and rows > LANES:
        rows //= 2
    out_shape, out_spec = ((c, r), pl.BlockSpec((c, rows), lambda i: (0, i))) if transpose else (
        (r, c), pl.BlockSpec((rows, c), lambda i: (i, 0)))
    return pl.pallas_call(
        functools.partial(_cast_kernel, transpose=transpose), grid=(r // rows,),
        in_specs=[pl.BlockSpec((None, rows, c), lambda i: (l, i, 0))],
        out_specs=out_spec, out_shape=jax.ShapeDtypeStruct(out_shape, BF16),
        compiler_params=_params("parallel"), name="cast_bf16",
    )(w)


def _addnorm_kernel(*refs, has_delta, emit_sum, emit_norm_f32, emit_norm_t):
    x = refs[0][...]
    k = 1
    if has_delta:
        x = x + refs[k][...].T
        k += 1
    g_ref = refs[k]
    outs = refs[k + 1:]
    y = x * lax.rsqrt(jnp.mean(x * x, axis=-1, keepdims=True) + NORM_EPS)
    y = y * g_ref[...]
    k = 0
    if emit_sum:
        outs[k][...] = x
        k += 1
    if emit_norm_f32:
        outs[k][...] = y
    else:
        outs[k][...] = y.astype(BF16)
    if emit_norm_t:
        outs[k + 1][...] = y.T.astype(BF16)


def _addnorm(x, gain, *, delta_t=None, emit_sum=False, emit_norm_f32=False, emit_norm_t=False,
             rows=256):
    t, d = x.shape
    rows = min(rows, t)
    spec = pl.BlockSpec((rows, d), lambda i: (i, 0))
    spec_t = pl.BlockSpec((d, rows), lambda i: (0, i))
    args, in_specs = [x], [spec]
    if delta_t is not None:
        args.append(delta_t)
        in_specs.append(spec_t)
    args.append(gain.reshape(1, d).astype(F32))
    in_specs.append(pl.BlockSpec((1, d), lambda i: (0, 0)))
    out_shapes, out_specs = [], []
    if emit_sum:
        out_shapes.append(jax.ShapeDtypeStruct((t, d), F32))
        out_specs.append(spec)
    out_shapes.append(jax.ShapeDtypeStruct((t, d), F32 if emit_norm_f32 else BF16))
    out_specs.append(spec)
    if emit_norm_t:
        out_shapes.append(jax.ShapeDtypeStruct((d, t), BF16))
        out_specs.append(spec_t)
    return pl.pallas_call(
        functools.partial(_addnorm_kernel, has_delta=delta_t is not None, emit_sum=emit_sum,
                          emit_norm_f32=emit_norm_f32, emit_norm_t=emit_norm_t),
        grid=(t // rows,), in_specs=in_specs, out_specs=out_specs, out_shape=out_shapes,
        compiler_params=_params("parallel"), name="addnorm",
    )(*args)


def _mm_kernel(x_ref, w_ref, *rest, epilogue):
    o_ref = rest[-1]
    acc = jnp.dot(x_ref[...], w_ref[...], preferred_element_type=F32)
    if epilogue == "sigmoid":
        acc = jax.nn.sigmoid(acc)
    elif epilogue == "residual":
        acc = acc + rest[0][...]
    o_ref[...] = acc.astype(o_ref.dtype)


def _matmul(x, w, *, out_dtype, col0=0, n=None, epilogue=None, residual=None, tm=1024, tn=1024):
    t, k = x.shape
    n = w.shape[1] if n is None else n
    tm, tn = min(tm, t), min(tn, n)
    assert col0 % tn == 0 and n % tn == 0
    cb = col0 // tn
    in_specs = [pl.BlockSpec((tm, k), lambda i, j: (i, 0)),
                pl.BlockSpec((k, tn), lambda i, j: (0, j + cb))]
    args = [x, w]
    if epilogue == "residual":
        in_specs.append(pl.BlockSpec((tm, tn), lambda i, j: (i, j)))
        args.append(residual)
    return pl.pallas_call(
        functools.partial(_mm_kernel, epilogue=epilogue),
        grid=(t // tm, n // tn), in_specs=in_specs,
        out_specs=pl.BlockSpec((tm, tn), lambda i, j: (i, j)),
        out_shape=jax.ShapeDtypeStruct((t, n), out_dtype),
        compiler_params=_params("parallel", "parallel"), name="matmul",
    )(*args)


def _glu_kernel(z_ref, wa_ref, wb_ref, o_ref):
    z = z_ref[...]
    a = jnp.dot(z, wa_ref[...], preferred_element_type=F32)
    b = jnp.dot(z, wb_ref[...], preferred_element_type=F32)
    o_ref[...] = (a * jax.nn.sigmoid(b)).astype(o_ref.dtype)


def _glu(z, w_glu, *, tm=1024, tn=512):
    t, k = z.shape
    n = w_glu.shape[1] // 2
    tm, tn = min(tm, t), min(tn, n)
    nb = n // tn
    return pl.pallas_call(
        _glu_kernel, grid=(t // tm, nb),
        in_specs=[pl.BlockSpec((tm, k), lambda i, j: (i, 0)),
                  pl.BlockSpec((k, tn), lambda i, j: (0, j)),
                  pl.BlockSpec((k, tn), lambda i, j: (0, j + nb))],
        out_specs=pl.BlockSpec((tm, tn), lambda i, j: (i, j)),
        out_shape=jax.ShapeDtypeStruct((t, n), BF16),
        compiler_params=_params("parallel", "parallel"), name="glu",
    )(z, w_glu, w_glu)


def _merge_kernel(a_ref, s_ref, ga_ref, gs_ref, wa_ref, ws_ref, o_ref):
    a = jnp.dot(a_ref[...], wa_ref[...], preferred_element_type=F32)
    s = jnp.dot(s_ref[...], ws_ref[...], preferred_element_type=F32)
    o_ref[...] = (ga_ref[...].astype(F32) * a + gs_ref[...].astype(F32) * s).astype(o_ref.dtype)


def _merge(attn, ssm, gates, w_ab, w_sb, *, tm=1024, tn=1024):
    t = attn.shape[0]
    n = w_ab.shape[1]
    tm, tn = min(tm, t), min(tn, n)
    nb = n // tn
    return pl.pallas_call(
        _merge_kernel, grid=(t // tm, nb),
        in_specs=[pl.BlockSpec((tm, attn.shape[1]), lambda i, j: (i, 0)),
                  pl.BlockSpec((tm, ssm.shape[1]), lambda i, j: (i, 0)),
                  pl.BlockSpec((tm, tn), lambda i, j: (i, j)),
                  pl.BlockSpec((tm, tn), lambda i, j: (i, j + nb)),
                  pl.BlockSpec((w_ab.shape[0], tn), lambda i, j: (0, j)),
                  pl.BlockSpec((w_sb.shape[0], tn), lambda i, j: (0, j))],
        out_specs=pl.BlockSpec((tm, tn), lambda i, j: (i, j)),
        out_shape=jax.ShapeDtypeStruct((t, n), BF16),
        compiler_params=_params("parallel", "parallel"), name="merge",
    )(attn, ssm, gates, gates, w_ab, w_sb)


def _alibi_slopes():
    return [float(v) for v in
            np.asarray(2.0 ** (-8.0 * np.arange(1, N_Q_HEADS + 1) / N_Q_HEADS), dtype=np.float32)]


def _attn_kernel(sink_ref, q0_ref, q1_ref, q2_ref, q3_ref, kc_ref, kp_ref, vc_ref, vp_ref, o_ref):
    n = pl.program_id(1)
    qi = lax.broadcasted_iota(jnp.int32, (BLOCK, 2 * BLOCK), 0)
    sj = lax.broadcasted_iota(jnp.int32, (BLOCK, 2 * BLOCK), 1)
    dist = qi - sj + BLOCK
    valid = (dist >= 0) & (dist < WINDOW) & ((n > 0) | (sj >= BLOCK))
    distf = dist.astype(F32)
    slopes = _alibi_slopes()
    scale = HEAD_DIM ** -0.5
    for k, q_ref in enumerate((q0_ref, q1_ref, q2_ref, q3_ref)):
        ksl = slice(k * HEAD_DIM, (k + 1) * HEAD_DIM)
        kw = jnp.concatenate([kp_ref[:, ksl], kc_ref[:, ksl]], axis=0)
        vw = jnp.concatenate([vp_ref[:, ksl], vc_ref[:, ksl]], axis=0)
        q4 = jnp.concatenate([q_ref[:, g * HEAD_DIM:(g + 1) * HEAD_DIM]
                              for g in range(Q_PER_KV)], axis=0)
        logits = lax.dot_general(q4, kw, (((1,), (1,)), ((), ())),
                                 preferred_element_type=F32) * scale
        ps, dens = [], []
        for g in range(Q_PER_KV):
            h = k * Q_PER_KV + g
            lg = logits[g * BLOCK:(g + 1) * BLOCK]
            lg = jnp.where(valid, lg - slopes[h] * distf, NEG_INF)
            sink = sink_ref[h]
            m = jnp.maximum(jnp.max(lg, axis=-1, keepdims=True), sink)
            p = jnp.exp(lg - m)
            dens.append(jnp.sum(p, axis=-1, keepdims=True) + jnp.exp(sink - m))
            ps.append(p.astype(BF16))
        pv = jnp.dot(jnp.concatenate(ps, axis=0), vw, preferred_element_type=F32)
        for g in range(Q_PER_KV):
            h = k * Q_PER_KV + g
            o_ref[:, h * HEAD_DIM:(h + 1) * HEAD_DIM] = (
                pv[g * BLOCK:(g + 1) * BLOCK] / dens[g]).astype(o_ref.dtype)


def _attention(qkv, sinks, *, bsz, seq):
    aw = N_Q_HEADS * HEAD_DIM
    kvw = N_KV_HEADS * HEAD_DIM
    assert aw == N_KV_HEADS * kvw
    nb = seq // BLOCK
    koff, voff = aw // kvw, aw // kvw + 1
    q_specs = [pl.BlockSpec((BLOCK, kvw), lambda b, n, g=g: (b * nb + n, g))
               for g in range(N_KV_HEADS)]
    prev = lambda b, n: b * nb + jnp.maximum(n - 1, 0)
    return pl.pallas_call(
        _attn_kernel, grid=(bsz, nb),
        in_specs=[pl.BlockSpec(memory_space=pltpu.SMEM)] + q_specs + [
            pl.BlockSpec((BLOCK, kvw), lambda b, n: (b * nb + n, koff)),
            pl.BlockSpec((BLOCK, kvw), lambda b, n: (prev(b, n), koff)),
            pl.BlockSpec((BLOCK, kvw), lambda b, n: (b * nb + n, voff)),
            pl.BlockSpec((BLOCK, kvw), lambda b, n: (prev(b, n), voff))],
        out_specs=pl.BlockSpec((BLOCK, aw), lambda b, n: (b * nb + n, 0)),
        out_shape=jax.ShapeDtypeStruct((bsz * seq, aw), BF16),
        compiler_params=_params("parallel", "parallel"), name="swa_attention",
    )(sinks.astype(F32), *([qkv] * (N_KV_HEADS + 4)))


S5_SLICE_GROUPS = 16
S5_SLICE_IN = S5_SLICE_GROUPS * SSM_GROUP
S5_SLICE_STATE = S5_SLICE_GROUPS * SSM_STATE


def _s5_kernel(u_ref, bd_ref, lam_ref, cd_ref, d_ref, z_ref, bu_ref, st_ref, *, steps):
    ns = S5_SLICE_STATE

    @pl.when(pl.program_id(1) == 0)
    def _():
        st_ref[...] = jnp.zeros_like(st_ref)

    u = u_ref[...]
    bu_ref[...] = jnp.dot(u.astype(BF16), bd_ref[0], preferred_element_type=F32)
    lr = jnp.broadcast_to(lam_ref[0, 0:1, :], (SUBLANES, ns))
    li = jnp.broadcast_to(lam_ref[0, 1:2, :], (SUBLANES, ns))

    def step(t, carry):
        xr, xi = carry
        row = pl.multiple_of(t * SUBLANES, SUBLANES)
        nr = lr * xr - li * xi + bu_ref[pl.ds(row, SUBLANES), 0:ns]
        ni = lr * xi + li * xr + bu_ref[pl.ds(row, SUBLANES), ns:2 * ns]
        bu_ref[pl.ds(row, SUBLANES), 0:ns] = nr
        bu_ref[pl.ds(row, SUBLANES), ns:2 * ns] = ni
        return nr, ni

    xr, xi = lax.fori_loop(0, steps, step, (st_ref[:, 0:ns], st_ref[:, ns:2 * ns]), unroll=4)
    st_ref[:, 0:ns] = xr
    st_ref[:, ns:2 * ns] = xi
    y = jnp.dot(bu_ref[...].astype(BF16), cd_ref[0], preferred_element_type=F32)
    y = y + d_ref[0] * u
    z_ref[...] = jax.nn.gelu(y).astype(z_ref.dtype)


def _s5_prepare(a_re, a_im, log_step, b_re, b_im, c_re, c_im):
    g = a_re.shape[0]
    ns = g // S5_SLICE_GROUPS
    ar, ai = a_re.astype(F32), a_im.astype(F32)
    dt = jnp.exp(log_step.astype(F32))[:, None]
    mag = jnp.exp(ar * dt)
    lr = mag * jnp.cos(ai * dt)
    li = mag * jnp.sin(ai * dt)
    den = ar * ar + ai * ai
    nr = lr - 1.0
    zr = (nr * ar + li * ai) / den
    zi = (li * ar - nr * ai) / den
    br, bi = b_re.astype(F32), b_im.astype(F32)
    bbar_r = zr[..., None] * br - zi[..., None] * bi
    bbar_i = zr[..., None] * bi + zi[..., None] * br
    eye = jnp.eye(S5_SLICE_GROUPS, dtype=F32)

    def bdiag(m):
        m = m.reshape(ns, S5_SLICE_GROUPS, SSM_STATE, SSM_GROUP)
        return jnp.einsum("ngph,gk->nghkp", m, eye).reshape(ns, S5_SLICE_IN, S5_SLICE_STATE)

    def cdiag(m):
        m = m.reshape(ns, S5_SLICE_GROUPS, SSM_GROUP, SSM_STATE)
        return jnp.einsum("nghp,gk->nkpgh", m, eye).reshape(ns, S5_SLICE_STATE, S5_SLICE_IN)

    bd = jnp.concatenate([bdiag(bbar_r), bdiag(bbar_i)], axis=-1).astype(BF16)
    cd = jnp.concatenate([cdiag(c_re.astype(F32)), -cdiag(c_im.astype(F32))], axis=1).astype(BF16)
    lam = jnp.stack([lr.reshape(ns, S5_SLICE_STATE), li.reshape(ns, S5_SLICE_STATE)], axis=1)
    return bd, lam, cd


def _s5(u, prep, d_skip, *, bsz, seq, steps=128):
    assert bsz == SUBLANES
    bd, lam, cd = prep
    ns = bd.shape[0]
    steps = min(steps, seq)
    rows = steps * bsz
    return pl.pallas_call(
        functools.partial(_s5_kernel, steps=steps),
        grid=(ns, seq // steps),
        in_specs=[pl.BlockSpec((rows, S5_SLICE_IN), lambda j, c: (c, j)),
                  pl.BlockSpec((1, S5_SLICE_IN, 2 * S5_SLICE_STATE), lambda j, c: (j, 0, 0)),
                  pl.BlockSpec((1, 2, S5_SLICE_STATE), lambda j, c: (j, 0, 0)),
                  pl.BlockSpec((1, 2 * S5_SLICE_STATE, S5_SLICE_IN), lambda j, c: (j, 0, 0)),
                  pl.BlockSpec((1, 1, S5_SLICE_IN), lambda j, c: (j, 0, 0))],
        out_specs=pl.BlockSpec((rows, S5_SLICE_IN), lambda j, c: (c, j)),
        out_shape=jax.ShapeDtypeStruct(u.shape, BF16),
        scratch_shapes=[pltpu.VMEM((rows, 2 * S5_SLICE_STATE), F32),
                        pltpu.VMEM((SUBLANES, 2 * S5_SLICE_STATE), F32)],
        compiler_params=_params("parallel", "arbitrary"), name="s5_scan",
    )(u, bd, lam, cd, d_skip.astype(F32).reshape(ns, 1, S5_SLICE_IN))


ROUTE_KEEP = PEER_TOPK + 1
ROUTE_ROWS = 24


def _sorting_network(n):
    pairs = []
    p = 1
    while p < n:
        k = p
        while k >= 1:
            for j in range(k % p, n - k, 2 * k):
                for i in range(min(k, n - j - k)):
                    if (i + j) // (2 * p) == (i + j + k) // (2 * p):
                        pairs.append((i + j, i + j + k))
            k //= 2
        p *= 2
    return pairs


def _extract_sorted(s, count):
    n = s.shape[0] // SUBLANES
    v = [s[r * SUBLANES:(r + 1) * SUBLANES] for r in range(n)]
    wires = 1 << (n - 1).bit_length()
    v += [None] * (wires - n)
    for i, j in _sorting_network(wires):
        if v[j] is None:
            continue
        if v[i] is None:
            v[i], v[j] = v[j], None
        else:
            v[i], v[j] = jnp.maximum(v[i], v[j]), jnp.minimum(v[i], v[j])
    v = v[:n]
    sub = lax.broadcasted_iota(jnp.int32, v[0].shape, 0).astype(F32)
    out = []
    for k in range(count):
        m = jnp.max(v[0], axis=0, keepdims=True)
        out.append(m)
        first = jnp.min(jnp.where(v[0] == m, sub, float(SUBLANES)), axis=0, keepdims=True)
        pop = sub == first
        for r in range(min(n, count - k - 1)):
            nxt = v[r + 1] if r + 1 < n else -jnp.inf
            v[r] = jnp.where(pop, nxt, v[r])
    return out


def _route_kernel(q_ref, keys_ref, theta_ref, coef_ref, s2_ref, e2_ref, a_scr, b_scr, *, tt):
    for h in range(PEER_HEADS):
        subs = []
        for c in range(2):
            qh = q_ref[:, (2 * h + c) * N_KEYS:(2 * h + c + 1) * N_KEYS].astype(BF16)
            subs.append(lax.dot_general(keys_ref[h, c], qh, (((1,), (1,)), ((), ())),
                                        preferred_element_type=F32))
        s1, s2 = subs
        a_scr[...] = jnp.full(a_scr.shape, -jnp.inf, F32)
        b_scr[...] = jnp.full(b_scr.shape, -jnp.inf, F32)
        a_rows = _extract_sorted(s1, ROUTE_KEEP)
        b_rows = _extract_sorted(s2, ROUTE_KEEP)
        for r in range(ROUTE_KEEP):
            a_scr[r:r + 1, :] = a_rows[r]
            b_scr[r:r + 1, :] = b_rows[r]
        cand = [a_rows[0] + b_scr[...]]
        for k in range(1, SUBLANES):
            cand.append(a_rows[k] + b_scr[0:SUBLANES, :])
        cand.append(a_scr[SUBLANES:ROUTE_ROWS, :] + b_rows[0])
        top = _extract_sorted(jnp.concatenate(cand, axis=0), ROUTE_KEEP)
        zsum = jnp.zeros((1, tt), F32)
        for r in range(PEER_TOPK):
            zsum = zsum + jnp.exp(top[r] - top[0])
        thr = 0.5 * (top[PEER_TOPK - 1] + top[PEER_TOPK])
        theta_ref[h] = thr - s1
        coef_ref[h] = jnp.exp(s1 - a_rows[0]) / zsum
        s2_ref[h] = s2
        e2_ref[h] = jnp.exp(s2 - b_rows[0])


def _route(qp, keys, *, tt=256):
    t = qp.shape[0]
    tt = min(tt, t)
    shp = jax.ShapeDtypeStruct((PEER_HEADS, N_KEYS, t), F32)
    ospec = pl.BlockSpec((PEER_HEADS, N_KEYS, tt), lambda i: (0, 0, i))
    return pl.pallas_call(
        functools.partial(_route_kernel, tt=tt), grid=(t // tt,),
        in_specs=[pl.BlockSpec((tt, qp.shape[1]), lambda i: (i, 0)),
                  pl.BlockSpec(keys.shape, lambda i: (0, 0, 0, 0))],
        out_specs=[ospec] * 4, out_shape=[shp] * 4,
        scratch_shapes=[pltpu.VMEM((ROUTE_ROWS, tt), F32), pltpu.VMEM((ROUTE_ROWS, tt), F32)],
        compiler_params=_params("parallel"), name="peer_route",
    )(qp, keys)


def _peer_kernel(xt_ref, u_ref, vt_ref, theta_ref, coef_ref, s2_ref, e2_ref, o_ref,
                 ht_scr, ht_next, hg_scr, *, eb, tm, n_blk):
    s = pl.program_id(0)

    @pl.when(s == 0)
    def _():
        ht_scr[...] = jnp.zeros_like(ht_scr)

    @pl.when(lax.rem(jnp.maximum(s - 1, 0), n_blk) == 0)
    def _():
        o_ref[...] = jnp.zeros_like(o_ref)

    n_half = 2
    for c in range(n_half):
        ts = slice(c * (tm // n_half), (c + 1) * (tm // n_half))
        ht_next[:, ts] = jnp.dot(u_ref[...], xt_ref[:, ts], preferred_element_type=F32)
        es = slice(c * (eb // n_half), (c + 1) * (eb // n_half))
        for ii in range(es.start // N_KEYS, es.stop // N_KEYS):
            rs = slice(ii * N_KEYS, (ii + 1) * N_KEYS)
            for lc in range(tm // LANES):
                ls = slice(lc * LANES, (lc + 1) * LANES)
                g = jnp.zeros((N_KEYS, LANES), F32)
                for h in range(PEER_HEADS):
                    th = theta_ref[h, 0, ii:ii + 1, ls]
                    cf = coef_ref[h, 0, ii:ii + 1, ls]
                    g = g + jnp.where(s2_ref[h, :, ls] >= th, e2_ref[h, :, ls] * cf, 0.0)
                hg_scr[rs, ls] = (jax.nn.gelu(ht_scr[rs, ls]) * g).astype(BF16)
        o_ref[...] += jnp.dot(vt_ref[:, es], hg_scr[es, :], preferred_element_type=F32)
    ht_scr[...] = ht_next[...]


def _peer(xnt, u_tab, vt_tab, route, *, tm=512, eb=512):
    d, t = xnt.shape
    n_blk = u_tab.shape[0] // eb
    tm = min(tm, t)
    n_i = eb // N_KEYS
    theta, coef, s2, e2 = route
    theta = theta.reshape(PEER_HEADS, N_KEYS // n_i, n_i, t)
    coef = coef.reshape(PEER_HEADS, N_KEYS // n_i, n_i, t)
    n_pairs = (t // tm) * n_blk
    cur = lambda s: jnp.minimum(s, n_pairs - 1)
    prv = lambda s: jnp.maximum(s - 1, 0)
    tile = lambda p: p // n_blk
    blk = lambda p: p % n_blk
    ispec = pl.BlockSpec((PEER_HEADS, 1, n_i, tm), lambda s: (0, blk(prv(s)), 0, tile(prv(s))))
    rspec = pl.BlockSpec((PEER_HEADS, N_KEYS, tm), lambda s: (0, 0, tile(prv(s))))
    return pl.pallas_call(
        functools.partial(_peer_kernel, eb=eb, tm=tm, n_blk=n_blk), grid=(n_pairs + 1,),
        in_specs=[pl.BlockSpec((d, tm), lambda s: (0, tile(cur(s)))),
                  pl.BlockSpec((eb, d), lambda s: (blk(cur(s)), 0)),
                  pl.BlockSpec((d, eb), lambda s: (0, blk(prv(s)))),
                  ispec, ispec, rspec, rspec],
        out_specs=pl.BlockSpec((d, tm), lambda s: (0, tile(prv(s)))),
        out_shape=jax.ShapeDtypeStruct((d, t), F32),
        scratch_shapes=[pltpu.VMEM((eb, tm), F32), pltpu.VMEM((eb, tm), F32),
                        pltpu.VMEM((eb, tm), BF16)],
        compiler_params=_params("arbitrary"), name="peer_experts",
    )(xnt, u_tab, vt_tab, theta, coef, s2, e2)


def _to_time_major(a, bsz, seq):
    return a.reshape(bsz, seq, -1).transpose(1, 0, 2).reshape(seq * bsz, -1)


def _to_batch_major(a, bsz, seq):
    return a.reshape(seq, bsz, -1).transpose(1, 0, 2).reshape(bsz * seq, -1)


def kernel(x, ln_mix, w_in, ssm_a_re, ssm_a_im, ssm_log_step, ssm_b_re, ssm_b_im, ssm_c_re,
           ssm_c_im, ssm_d, w_glu, attn_sinks, w_attn_branch, w_ssm_branch, w_out, ln_ffn,
           w_peer_q, peer_keys, peer_u, peer_v, ln_final):
    bsz, seq, d = x.shape
    depth = w_in.shape[0]
    aw = N_Q_HEADS * HEAD_DIM
    qkv_w = aw + 2 * N_KV_HEADS * HEAD_DIM
    ssm_w = ssm_d.shape[-1]

    h = x.reshape(bsz * seq, d)
    pt = None
    for l in range(depth):
        if l == 0:
            xres = h
            (xn,) = _addnorm(h, ln_mix[l])
        else:
            xres, xn = _addnorm(h, ln_mix[l], delta_t=pt, emit_sum=True)
        w_l = _cast_bf16(w_in, l)
        qkv = _matmul(xn, w_l, n=qkv_w, out_dtype=BF16)
        u = _matmul(xn, w_l, col0=qkv_w, n=ssm_w, out_dtype=F32)
        gates = _matmul(xn, w_l, col0=qkv_w + ssm_w, n=2 * d, out_dtype=BF16, epilogue="sigmoid")
        attn = _attention(qkv, attn_sinks[l], bsz=bsz, seq=seq)
        prep = _s5_prepare(ssm_a_re[l], ssm_a_im[l], ssm_log_step[l], ssm_b_re[l], ssm_b_im[l],
                           ssm_c_re[l], ssm_c_im[l])
        z = _s5(_to_time_major(u, bsz, seq), prep, ssm_d[l], bsz=bsz, seq=seq)
        ssm = _glu(_to_batch_major(z, bsz, seq), _cast_bf16(w_glu, l))
        merged = _merge(attn, ssm, gates, _cast_bf16(w_attn_branch, l),
                        _cast_bf16(w_ssm_branch, l))
        h = _matmul(merged, _cast_bf16(w_out, l), out_dtype=F32, epilogue="residual",
                    residual=xres)
        hn, hnt = _addnorm(h, ln_ffn[l], emit_norm_t=True)
        qp = _matmul(hn, _cast_bf16(w_peer_q, l), out_dtype=F32)
        route = _route(qp, peer_keys[l].astype(BF16))
        pt = _peer(hnt, _cast_bf16(peer_u, l), _cast_bf16(peer_v, l, transpose=True), route)
    (out,) = _addnorm(h, ln_final, delta_t=pt, emit_norm_f32=True)
    return out.reshape(bsz, seq, d)
```

```python
import functools

import numpy as np
import jax
import jax.numpy as jnp
from jax import lax
from jax.experimental import pallas as pl
from jax.experimental.pallas import tpu as pltpu

F32 = jnp.float32
BF16 = jnp.bfloat16

HEAD_DIM = 128
N_Q_HEADS = 16
N_KV_HEADS = 4
Q_PER_KV = 4
WINDOW = 128
BLOCK = 128
NEG_INF = -1e30
SSM_GROUP = 16
SSM_STATE = 64
PEER_HEADS = 8
N_KEYS = 128
PEER_TOPK = 16
NORM_EPS = 1e-6

V7X_VMEM_BYTES = 64 * 1024 * 1024
VMEM_LIMIT = V7X_VMEM_BYTES - 8 * 1024 * 1024
V7X_MXU_DIM = 256
SUBLANES = 8
LANES = 128


def _params(*sem, flags=None):
    return pltpu.CompilerParams(dimension_semantics=sem, vmem_limit_bytes=VMEM_LIMIT, flags=flags)


CAST_BLOCK_ELEMS = 2 * 1024 * 1024


def _cast_kernel(w_ref, o_ref, *, transpose):
    w = w_ref[...]
    o_ref[...] = (w.T if transpose else w).astype(o_ref.dtype)


def _cast_bf16(w, l, *, transpose=False):
    _, r, c = w.shape
    rows = r
    while rows * c > CAST_BLOCK_ELEMS and rows % 2 == 0 and rows > LANES:
        rows //= 2
    out_shape, out_spec = ((c, r), pl.BlockSpec((c, rows), lambda i: (0, i))) if transpose else (
        (r, c), pl.BlockSpec((rows, c), lambda i: (i, 0)))
    return pl.pallas_call(
        functools.partial(_cast_kernel, transpose=transpose), grid=(r // rows,),
        in_specs=[pl.BlockSpec((None, rows, c), lambda i: (l, i, 0))],
        out_specs=out_spec, out_shape=jax.ShapeDtypeStruct(out_shape, BF16),
        compiler_params=_params("parallel"), name="cast_bf16",
    )(w)


def _addnorm_kernel(*refs, has_delta, emit_sum, emit_norm_f32, emit_norm_t):
    x = refs[0][...]
    k = 1
    if has_delta:
        x = x + refs[k][...].T
        k += 1
    g_ref = refs[k]
    outs = refs[k + 1:]
    y = x * lax.rsqrt(jnp.mean(x * x, axis=-1, keepdims=True) + NORM_EPS)
    y = y * g_ref[...]
    k = 0
    if emit_sum:
        outs[k][...] = x
        k += 1
    if emit_norm_f32:
        outs[k][...] = y
    else:
        outs[k][...] = y.astype(BF16)
    if emit_norm_t:
        outs[k + 1][...] = y.T.astype(BF16)


def _addnorm(x, gain, *, delta_t=None, emit_sum=False, emit_norm_f32=False, emit_norm_t=False,
             rows=256):
    t, d = x.shape
    rows = min(rows, t)
    spec = pl.BlockSpec((rows, d), lambda i: (i, 0))
    spec_t = pl.BlockSpec((d, rows), lambda i: (0, i))
    args, in_specs = [x], [spec]
    if delta_t is not None:
        args.append(delta_t)
        in_specs.append(spec_t)
    args.append(gain.reshape(1, d).astype(F32))
    in_specs.append(pl.BlockSpec((1, d), lambda i: (0, 0)))
    out_shapes, out_specs = [], []
    if emit_sum:
        out_shapes.append(jax.ShapeDtypeStruct((t, d), F32))
        out_specs.append(spec)
    out_shapes.append(jax.ShapeDtypeStruct((t, d), F32 if emit_norm_f32 else BF16))
    out_specs.append(spec)
    if emit_norm_t:
        out_shapes.append(jax.ShapeDtypeStruct((d, t), BF16))
        out_specs.append(spec_t)
    return pl.pallas_call(
        functools.partial(_addnorm_kernel, has_delta=delta_t is not None, emit_sum=emit_sum,
                          emit_norm_f32=emit_norm_f32, emit_norm_t=emit_norm_t),
        grid=(t // rows,), in_specs=in_specs, out_specs=out_specs, out_shape=out_shapes,
        compiler_params=_params("parallel"), name="addnorm",
    )(*args)


def _mm_kernel(x_ref, w_ref, *rest, epilogue):
    o_ref = rest[-1]
    acc = jnp.dot(x_ref[...], w_ref[...], preferred_element_type=F32)
    if epilogue == "sigmoid":
        acc = jax.nn.sigmoid(acc)
    elif epilogue == "residual":
        acc = acc + rest[0][...]
    o_ref[...] = acc.astype(o_ref.dtype)


def _matmul(x, w, *, out_dtype, col0=0, n=None, epilogue=None, residual=None, tm=1024, tn=1024):
    t, k = x.shape
    n = w.shape[1] if n is None else n
    tm, tn = min(tm, t), min(tn, n)
    assert col0 % tn == 0 and n % tn == 0
    cb = col0 // tn
    in_specs = [pl.BlockSpec((tm, k), lambda i, j: (i, 0)),
                pl.BlockSpec((k, tn), lambda i, j: (0, j + cb))]
    args = [x, w]
    if epilogue == "residual":
        in_specs.append(pl.BlockSpec((tm, tn), lambda i, j: (i, j)))
        args.append(residual)
    return pl.pallas_call(
        functools.partial(_mm_kernel, epilogue=epilogue),
        grid=(t // tm, n // tn), in_specs=in_specs,
        out_specs=pl.BlockSpec((tm, tn), lambda i, j: (i, j)),
        out_shape=jax.ShapeDtypeStruct((t, n), out_dtype),
        compiler_params=_params("parallel", "parallel"), name="matmul",
    )(*args)


def _glu_kernel(z_ref, wa_ref, wb_ref, o_ref):
    z = z_ref[...]
    a = jnp.dot(z, wa_ref[...], preferred_element_type=F32)
    b = jnp.dot(z, wb_ref[...], preferred_element_type=F32)
    o_ref[...] = (a * jax.nn.sigmoid(b)).astype(o_ref.dtype)


def _glu(z, w_glu, *, tm=1024, tn=512):
    t, k = z.shape
    n = w_glu.shape[1] // 2
    tm, tn = min(tm, t), min(tn, n)
    nb = n // tn
    return pl.pallas_call(
        _glu_kernel, grid=(t // tm, nb),
        in_specs=[pl.BlockSpec((tm, k), lambda i, j: (i, 0)),
                  pl.BlockSpec((k, tn), lambda i, j: (0, j)),
                  pl.BlockSpec((k, tn), lambda i, j: (0, j + nb))],
        out_specs=pl.BlockSpec((tm, tn), lambda i, j: (i, j)),
        out_shape=jax.ShapeDtypeStruct((t, n), BF16),
        compiler_params=_params("parallel", "parallel"), name="glu",
    )(z, w_glu, w_glu)


def _merge_kernel(a_ref, s_ref, ga_ref, gs_ref, wa_ref, ws_ref, o_ref):
    a = jnp.dot(a_ref[...], wa_ref[...], preferred_element_type=F32)
    s = jnp.dot(s_ref[...], ws_ref[...], preferred_element_type=F32)
    o_ref[...] = (ga_ref[...].astype(F32) * a + gs_ref[...].astype(F32) * s).astype(o_ref.dtype)


def _merge(attn, ssm, gates, w_ab, w_sb, *, tm=1024, tn=1024):
    t = attn.shape[0]
    n = w_ab.shape[1]
    tm, tn = min(tm, t), min(tn, n)
    nb = n // tn
    return pl.pallas_call(
        _merge_kernel, grid=(t // tm, nb),
        in_specs=[pl.BlockSpec((tm, attn.shape[1]), lambda i, j: (i, 0)),
                  pl.BlockSpec((tm, ssm.shape[1]), lambda i, j: (i, 0)),
                  pl.BlockSpec((tm, tn), lambda i, j: (i, j)),
                  pl.BlockSpec((tm, tn), lambda i, j: (i, j + nb)),
                  pl.BlockSpec((w_ab.shape[0], tn), lambda i, j: (0, j)),
                  pl.BlockSpec((w_sb.shape[0], tn), lambda i, j: (0, j))],
        out_specs=pl.BlockSpec((tm, tn), lambda i, j: (i, j)),
        out_shape=jax.ShapeDtypeStruct((t, n), BF16),
        compiler_params=_params("parallel", "parallel"), name="merge",
    )(attn, ssm, gates, gates, w_ab, w_sb)


def _alibi_slopes():
    return [float(v) for v in
            np.asarray(2.0 ** (-8.0 * np.arange(1, N_Q_HEADS + 1) / N_Q_HEADS), dtype=np.float32)]


def _attn_kernel(sink_ref, q0_ref, q1_ref, q2_ref, q3_ref, kc_ref, kp_ref, vc_ref, vp_ref, o_ref):
    n = pl.program_id(1)
    qi = lax.broadcasted_iota(jnp.int32, (BLOCK, 2 * BLOCK), 0)
    sj = lax.broadcasted_iota(jnp.int32, (BLOCK, 2 * BLOCK), 1)
    dist = qi - sj + BLOCK
    valid = (dist >= 0) & (dist < WINDOW) & ((n > 0) | (sj >= BLOCK))
    distf = dist.astype(F32)
    slopes = _alibi_slopes()
    scale = HEAD_DIM ** -0.5
    for k, q_ref in enumerate((q0_ref, q1_ref, q2_ref, q3_ref)):
        ksl = slice(k * HEAD_DIM, (k + 1) * HEAD_DIM)
        kw = jnp.concatenate([kp_ref[:, ksl], kc_ref[:, ksl]], axis=0)
        vw = jnp.concatenate([vp_ref[:, ksl], vc_ref[:, ksl]], axis=0)
        q4 = jnp.concatenate([q_ref[:, g * HEAD_DIM:(g + 1) * HEAD_DIM]
                              for g in range(Q_PER_KV)], axis=0)
        logits = lax.dot_general(q4, kw, (((1,), (1,)), ((), ())),
                                 preferred_element_type=F32) * scale
        ps, dens = [], []
        for g in range(Q_PER_KV):
            h = k * Q_PER_KV + g
            lg = logits[g * BLOCK:(g + 1) * BLOCK]
            lg = jnp.where(valid, lg - slopes[h] * distf, NEG_INF)
            sink = sink_ref[h]
            m = jnp.maximum(jnp.max(lg, axis=-1, keepdims=True), sink)
            p = jnp.exp(lg - m)
            dens.append(jnp.sum(p, axis=-1, keepdims=True) + jnp.exp(sink - m))
            ps.append(p.astype(BF16))
        pv = jnp.dot(jnp.concatenate(ps, axis=0), vw, preferred_element_type=F32)
        for g in range(Q_PER_KV):
            h = k * Q_PER_KV + g
            o_ref[:, h * HEAD_DIM:(h + 1) * HEAD_DIM] = (
                pv[g * BLOCK:(g + 1) * BLOCK] / dens[g]).astype(o_ref.dtype)


def _attention(qkv, sinks, *, bsz, seq):
    aw = N_Q_HEADS * HEAD_DIM
    kvw = N_KV_HEADS * HEAD_DIM
    assert aw == N_KV_HEADS * kvw
    nb = seq // BLOCK
    koff, voff = aw // kvw, aw // kvw + 1
    q_specs = [pl.BlockSpec((BLOCK, kvw), lambda b, n, g=g: (b * nb + n, g))
               for g in range(N_KV_HEADS)]
    prev = lambda b, n: b * nb + jnp.maximum(n - 1, 0)
    return pl.pallas_call(
        _attn_kernel, grid=(bsz, nb),
        in_specs=[pl.BlockSpec(memory_space=pltpu.SMEM)] + q_specs + [
            pl.BlockSpec((BLOCK, kvw), lambda b, n: (b * nb + n, koff)),
            pl.BlockSpec((BLOCK, kvw), lambda b, n: (prev(b, n), koff)),
            pl.BlockSpec((BLOCK, kvw), lambda b, n: (b * nb + n, voff)),
            pl.BlockSpec((BLOCK, kvw), lambda b, n: (prev(b, n), voff))],
        out_specs=pl.BlockSpec((BLOCK, aw), lambda b, n: (b * nb + n, 0)),
        out_shape=jax.ShapeDtypeStruct((bsz * seq, aw), BF16),
        compiler_params=_params("parallel", "parallel"), name="swa_attention",
    )(sinks.astype(F32), *([qkv] * (N_KV_HEADS + 4)))


S5_SLICE_GROUPS = 16
S5_SLICE_IN = S5_SLICE_GROUPS * SSM_GROUP
S5_SLICE_STATE = S5_SLICE_GROUPS * SSM_STATE


def _s5_kernel(u_ref, bd_ref, lam_ref, cd_ref, d_ref, z_ref, bu_ref, st_ref, tb_ref, *, steps):
    ns = S5_SLICE_STATE
    n_slab = S5_SLICE_IN // LANES

    @pl.when(pl.program_id(1) == 0)
    def _():
        st_ref[...] = jnp.zeros_like(st_ref)

    for b in range(SUBLANES):
        for sl in range(n_slab):
            tb_ref[sl, pl.ds(b, steps, stride=SUBLANES), :] = u_ref[b, :, sl * LANES:(sl + 1) * LANES]
    u = jnp.concatenate([tb_ref[sl] for sl in range(n_slab)], axis=-1)
    bu_ref[...] = jnp.dot(u.astype(BF16), bd_ref[0], preferred_element_type=F32)
    lr = jnp.broadcast_to(lam_ref[0, 0:1, :], (SUBLANES, ns))
    li = jnp.broadcast_to(lam_ref[0, 1:2, :], (SUBLANES, ns))

    def step(t, carry):
        xr, xi = carry
        row = pl.multiple_of(t * SUBLANES, SUBLANES)
        nr = lr * xr - li * xi + bu_ref[pl.ds(row, SUBLANES), 0:ns]
        ni = lr * xi + li * xr + bu_ref[pl.ds(row, SUBLANES), ns:2 * ns]
        bu_ref[pl.ds(row, SUBLANES), 0:ns] = nr
        bu_ref[pl.ds(row, SUBLANES), ns:2 * ns] = ni
        return nr, ni

    xr, xi = lax.fori_loop(0, steps, step, (st_ref[:, 0:ns], st_ref[:, ns:2 * ns]), unroll=4)
    st_ref[:, 0:ns] = xr
    st_ref[:, ns:2 * ns] = xi
    y = jnp.dot(bu_ref[...].astype(BF16), cd_ref[0], preferred_element_type=F32)
    z = jax.nn.gelu(y + d_ref[0] * u)
    for sl in range(n_slab):
        tb_ref[sl] = z[:, sl * LANES:(sl + 1) * LANES]
    for b in range(SUBLANES):
        for sl in range(n_slab):
            z_ref[b, :, sl * LANES:(sl + 1) * LANES] = (
                tb_ref[sl, pl.ds(b, steps, stride=SUBLANES), :].astype(z_ref.dtype))


def _s5_prepare(a_re, a_im, log_step, b_re, b_im, c_re, c_im):
    g = a_re.shape[0]
    ns = g // S5_SLICE_GROUPS
    ar, ai = a_re.astype(F32), a_im.astype(F32)
    dt = jnp.exp(log_step.astype(F32))[:, None]
    mag = jnp.exp(ar * dt)
    lr = mag * jnp.cos(ai * dt)
    li = mag * jnp.sin(ai * dt)
    den = ar * ar + ai * ai
    nr = lr - 1.0
    zr = (nr * ar + li * ai) / den
    zi = (li * ar - nr * ai) / den
    br, bi = b_re.astype(F32), b_im.astype(F32)
    bbar_r = zr[..., None] * br - zi[..., None] * bi
    bbar_i = zr[..., None] * bi + zi[..., None] * br
    eye = jnp.eye(S5_SLICE_GROUPS, dtype=F32)

    def bdiag(m):
        m = m.reshape(ns, S5_SLICE_GROUPS, SSM_STATE, SSM_GROUP)
        return jnp.einsum("ngph,gk->nghkp", m, eye).reshape(ns, S5_SLICE_IN, S5_SLICE_STATE)

    def cdiag(m):
        m = m.reshape(ns, S5_SLICE_GROUPS, SSM_GROUP, SSM_STATE)
        return jnp.einsum("nghp,gk->nkpgh", m, eye).reshape(ns, S5_SLICE_STATE, S5_SLICE_IN)

    bd = jnp.concatenate([bdiag(bbar_r), bdiag(bbar_i)], axis=-1).astype(BF16)
    cd = jnp.concatenate([cdiag(c_re.astype(F32)), -cdiag(c_im.astype(F32))], axis=1).astype(BF16)
    lam = jnp.stack([lr.reshape(ns, S5_SLICE_STATE), li.reshape(ns, S5_SLICE_STATE)], axis=1)
    return bd, lam, cd


def _s5(u, prep, d_skip, *, bsz, seq, steps=128):
    assert bsz == SUBLANES
    bd, lam, cd = prep
    ns = bd.shape[0]
    steps = min(steps, seq)
    rows = steps * bsz
    width = u.shape[1]
    blk = pl.BlockSpec((bsz, steps, S5_SLICE_IN), lambda j, c: (0, c, j))
    return pl.pallas_call(
        functools.partial(_s5_kernel, steps=steps),
        grid=(ns, seq // steps),
        in_specs=[blk,
                  pl.BlockSpec((1, S5_SLICE_IN, 2 * S5_SLICE_STATE), lambda j, c: (j, 0, 0)),
                  pl.BlockSpec((1, 2, S5_SLICE_STATE), lambda j, c: (j, 0, 0)),
                  pl.BlockSpec((1, 2 * S5_SLICE_STATE, S5_SLICE_IN), lambda j, c: (j, 0, 0)),
                  pl.BlockSpec((1, 1, S5_SLICE_IN), lambda j, c: (j, 0, 0))],
        out_specs=blk,
        out_shape=jax.ShapeDtypeStruct((bsz, seq, width), BF16),
        scratch_shapes=[pltpu.VMEM((rows, 2 * S5_SLICE_STATE), F32),
                        pltpu.VMEM((SUBLANES, 2 * S5_SLICE_STATE), F32),
                        pltpu.VMEM((S5_SLICE_IN // LANES, rows, LANES), F32)],
        compiler_params=_params("parallel", "arbitrary"), name="s5_scan",
    )(u.reshape(bsz, seq, width), bd, lam, cd,
      d_skip.astype(F32).reshape(ns, 1, S5_SLICE_IN)).reshape(bsz * seq, width)


ROUTE_KEEP = PEER_TOPK + 1
ROUTE_ROWS = 24


def _sorting_network(n):
    pairs = []
    p = 1
    while p < n:
        k = p
        while k >= 1:
            for j in range(k % p, n - k, 2 * k):
                for i in range(min(k, n - j - k)):
                    if (i + j) // (2 * p) == (i + j + k) // (2 * p):
                        pairs.append((i + j, i + j + k))
            k //= 2
        p *= 2
    return pairs


def _extract_sorted(s, count):
    n = s.shape[0] // SUBLANES
    v = [s[r * SUBLANES:(r + 1) * SUBLANES] for r in range(n)]
    wires = 1 << (n - 1).bit_length()
    v += [None] * (wires - n)
    for i, j in _sorting_network(wires):
        if v[j] is None:
            continue
        if v[i] is None:
            v[i], v[j] = v[j], None
        else:
            v[i], v[j] = jnp.maximum(v[i], v[j]), jnp.minimum(v[i], v[j])
    v = v[:n]
    sub = lax.broadcasted_iota(jnp.int32, v[0].shape, 0).astype(F32)
    out = []
    for k in range(count):
        m = jnp.max(v[0], axis=0, keepdims=True)
        out.append(m)
        first = jnp.min(jnp.where(v[0] == m, sub, float(SUBLANES)), axis=0, keepdims=True)
        pop = sub == first
        for r in range(min(n, count - k - 1)):
            nxt = v[r + 1] if r + 1 < n else -jnp.inf
            v[r] = jnp.where(pop, nxt, v[r])
    return out


def _route_kernel(q_ref, keys_ref, theta_ref, coef_ref, s2_ref, e2_ref, a_scr, b_scr, *, tt):
    for h in range(PEER_HEADS):
        subs = []
        for c in range(2):
            qh = q_ref[:, (2 * h + c) * N_KEYS:(2 * h + c + 1) * N_KEYS].astype(BF16)
            subs.append(lax.dot_general(keys_ref[h, c], qh, (((1,), (1,)), ((), ())),
                                        preferred_element_type=F32))
        s1, s2 = subs
        a_scr[...] = jnp.full(a_scr.shape, -jnp.inf, F32)
        b_scr[...] = jnp.full(b_scr.shape, -jnp.inf, F32)
        a_rows = _extract_sorted(s1, ROUTE_KEEP)
        b_rows = _extract_sorted(s2, ROUTE_KEEP)
        for r in range(ROUTE_KEEP):
            a_scr[r:r + 1, :] = a_rows[r]
            b_scr[r:r + 1, :] = b_rows[r]
        cand = [a_rows[0] + b_scr[...]]
        for k in range(1, SUBLANES):
            cand.append(a_rows[k] + b_scr[0:SUBLANES, :])
        cand.append(a_scr[SUBLANES:ROUTE_ROWS, :] + b_rows[0])
        top = _extract_sorted(jnp.concatenate(cand, axis=0), ROUTE_KEEP)
        zsum = jnp.zeros((1, tt), F32)
        for r in range(PEER_TOPK):
            zsum = zsum + jnp.exp(top[r] - top[0])
        thr = 0.5 * (top[PEER_TOPK - 1] + top[PEER_TOPK])
        theta_ref[h] = thr - s1
        coef_ref[h] = jnp.exp(s1 - a_rows[0]) / zsum
        s2_ref[h] = s2
        e2_ref[h] = jnp.exp(s2 - b_rows[0])


def _route(qp, keys, *, tt=256):
    t = qp.shape[0]
    tt = min(tt, t)
    shp = jax.ShapeDtypeStruct((PEER_HEADS, N_KEYS, t), F32)
    ospec = pl.BlockSpec((PEER_HEADS, N_KEYS, tt), lambda i: (0, 0, i))
    return pl.pallas_call(
        functools.partial(_route_kernel, tt=tt), grid=(t // tt,),
        in_specs=[pl.BlockSpec((tt, qp.shape[1]), lambda i: (i, 0)),
                  pl.BlockSpec(keys.shape, lambda i: (0, 0, 0, 0))],
        out_specs=[ospec] * 4, out_shape=[shp] * 4,
        scratch_shapes=[pltpu.VMEM((ROUTE_ROWS, tt), F32), pltpu.VMEM((ROUTE_ROWS, tt), F32)],
        compiler_params=_params("parallel"), name="peer_route",
    )(qp, keys)


def _peer_kernel(xt_ref, u_ref, vt_ref, theta_ref, coef_ref, s2_ref, e2_ref, o_ref,
                 ht_scr, ht_next, hg_scr, *, eb, tm, n_blk):
    s = pl.program_id(0)

    @pl.when(s == 0)
    def _():
        ht_scr[...] = jnp.zeros_like(ht_scr)

    @pl.when(lax.rem(jnp.maximum(s - 1, 0), n_blk) == 0)
    def _():
        o_ref[...] = jnp.zeros_like(o_ref)

    n_half = 2
    for c in range(n_half):
        ts = slice(c * (tm // n_half), (c + 1) * (tm // n_half))
        ht_next[:, ts] = jnp.dot(u_ref[...], xt_ref[:, ts], preferred_element_type=F32)
        es = slice(c * (eb // n_half), (c + 1) * (eb // n_half))
        for ii in range(es.start // N_KEYS, es.stop // N_KEYS):
            rs = slice(ii * N_KEYS, (ii + 1) * N_KEYS)
            for lc in range(tm // LANES):
                ls = slice(lc * LANES, (lc + 1) * LANES)
                g = jnp.zeros((N_KEYS, LANES), F32)
                for h in range(PEER_HEADS):
                    th = theta_ref[h, 0, ii:ii + 1, ls]
                    cf = coef_ref[h, 0, ii:ii + 1, ls]
                    g = g + jnp.where(s2_ref[h, :, ls] >= th, e2_ref[h, :, ls] * cf, 0.0)
                hg_scr[rs, ls] = (jax.nn.gelu(ht_scr[rs, ls]) * g).astype(BF16)
        o_ref[...] += jnp.dot(vt_ref[:, es], hg_scr[es, :], preferred_element_type=F32)
    ht_scr[...] = ht_next[...]


def _peer(xnt, u_tab, vt_tab, route, *, tm=512, eb=512):
    d, t = xnt.shape
    n_blk = u_tab.shape[0] // eb
    tm = min(tm, t)
    n_i = eb // N_KEYS
    theta, coef, s2, e2 = route
    theta = theta.reshape(PEER_HEADS, N_KEYS // n_i, n_i, t)
    coef = coef.reshape(PEER_HEADS, N_KEYS // n_i, n_i, t)
    n_pairs = (t // tm) * n_blk
    cur = lambda s: jnp.minimum(s, n_pairs - 1)
    prv = lambda s: jnp.maximum(s - 1, 0)
    tile = lambda p: p // n_blk
    blk = lambda p: p % n_blk
    ispec = pl.BlockSpec((PEER_HEADS, 1, n_i, tm), lambda s: (0, blk(prv(s)), 0, tile(prv(s))))
    rspec = pl.BlockSpec((PEER_HEADS, N_KEYS, tm), lambda s: (0, 0, tile(prv(s))))
    return pl.pallas_call(
        functools.partial(_peer_kernel, eb=eb, tm=tm, n_blk=n_blk), grid=(n_pairs + 1,),
        in_specs=[pl.BlockSpec((d, tm), lambda s: (0, tile(cur(s)))),
                  pl.BlockSpec((eb, d), lambda s: (blk(cur(s)), 0)),
                  pl.BlockSpec((d, eb), lambda s: (0, blk(prv(s)))),
                  ispec, ispec, rspec, rspec],
        out_specs=pl.BlockSpec((d, tm), lambda s: (0, tile(prv(s)))),
        out_shape=jax.ShapeDtypeStruct((d, t), F32),
        scratch_shapes=[pltpu.VMEM((eb, tm), F32), pltpu.VMEM((eb, tm), F32),
                        pltpu.VMEM((eb, tm), BF16)],
        compiler_params=_params("arbitrary"), name="peer_experts",
    )(xnt, u_tab, vt_tab, theta, coef, s2, e2)


def kernel(x, ln_mix, w_in, ssm_a_re, ssm_a_im, ssm_log_step, ssm_b_re, ssm_b_im, ssm_c_re,
           ssm_c_im, ssm_d, w_glu, attn_sinks, w_attn_branch, w_ssm_branch, w_out, ln_ffn,
           w_peer_q, peer_keys, peer_u, peer_v, ln_final):
    bsz, seq, d = x.shape
    depth = w_in.shape[0]
    aw = N_Q_HEADS * HEAD_DIM
    qkv_w = aw + 2 * N_KV_HEADS * HEAD_DIM
    ssm_w = ssm_d.shape[-1]

    h = x.reshape(bsz * seq, d)
    pt = None
    for l in range(depth):
        if l == 0:
            xres = h
            (xn,) = _addnorm(h, ln_mix[l])
        else:
            xres, xn = _addnorm(h, ln_mix[l], delta_t=pt, emit_sum=True)
        w_l = _cast_bf16(w_in, l)
        qkv = _matmul(xn, w_l, n=qkv_w, out_dtype=BF16)
        u = _matmul(xn, w_l, col0=qkv_w, n=ssm_w, out_dtype=F32)
        gates = _matmul(xn, w_l, col0=qkv_w + ssm_w, n=2 * d, out_dtype=BF16, epilogue="sigmoid")
        attn = _attention(qkv, attn_sinks[l], bsz=bsz, seq=seq)
        prep = _s5_prepare(ssm_a_re[l], ssm_a_im[l], ssm_log_step[l], ssm_b_re[l], ssm_b_im[l],
                           ssm_c_re[l], ssm_c_im[l])
        z = _s5(u, prep, ssm_d[l], bsz=bsz, seq=seq)
        ssm = _glu(z, _cast_bf16(w_glu, l))
        merged = _merge(attn, ssm, gates, _cast_bf16(w_attn_branch, l),
                        _cast_bf16(w_ssm_branch, l))
        h = _matmul(merged, _cast_bf16(w_out, l), out_dtype=F32, epilogue="residual",
                    residual=xres)
        hn, hnt = _addnorm(h, ln_ffn[l], emit_norm_t=True, rows=512)
        qp = _matmul(hn, _cast_bf16(w_peer_q, l), out_dtype=F32)
        route = _route(qp, peer_keys[l].astype(BF16))
        pt = _peer(hnt, _cast_bf16(peer_u, l), _cast_bf16(peer_v, l, transpose=True), route)
    (out,) = _addnorm(h, ln_final, delta_t=pt, emit_norm_f32=True)
    return out.reshape(bsz, seq, d)
```

```python
import functools

import numpy as np
import jax
import jax.numpy as jnp
from jax import lax
from jax.experimental import pallas as pl
from jax.experimental.pallas import tpu as pltpu

F32 = jnp.float32
BF16 = jnp.bfloat16

HEAD_DIM = 128
N_Q_HEADS = 16
N_KV_HEADS = 4
Q_PER_KV = 4
WINDOW = 128
BLOCK = 128
NEG_INF = -1e30
SSM_GROUP = 16
SSM_STATE = 64
PEER_HEADS = 8
N_KEYS = 128
PEER_TOPK = 16
NORM_EPS = 1e-6

V7X_VMEM_BYTES = 64 * 1024 * 1024
VMEM_LIMIT = V7X_VMEM_BYTES - 8 * 1024 * 1024
V7X_MXU_DIM = 256
SUBLANES = 8
LANES = 128


def _params(*sem, flags=None):
    return pltpu.CompilerParams(dimension_semantics=sem, vmem_limit_bytes=VMEM_LIMIT, flags=flags)


CAST_BLOCK_ELEMS = 2 * 1024 * 1024


def _cast_kernel(w_ref, o_ref, *, transpose):
    w = w_ref[...]
    o_ref[...] = (w.T if transpose else w).astype(o_ref.dtype)


def _cast_bf16(w, l, *, transpose=False):
    _, r, c = w.shape
    rows = r
    while rows * c > CAST_BLOCK_ELEMS and rows % 2 == 0 and rows > LANES:
        rows //= 2
    out_shape, out_spec = ((c, r), pl.BlockSpec((c, rows), lambda i: (0, i))) if transpose else (
        (r, c), pl.BlockSpec((rows, c), lambda i: (i, 0)))
    return pl.pallas_call(
        functools.partial(_cast_kernel, transpose=transpose), grid=(r // rows,),
        in_specs=[pl.BlockSpec((None, rows, c), lambda i: (l, i, 0))],
        out_specs=out_spec, out_shape=jax.ShapeDtypeStruct(out_shape, BF16),
        compiler_params=_params("parallel"), name="cast_bf16",
    )(w)


def _addnorm_kernel(*refs, has_delta, emit_sum, emit_norm_f32, emit_norm_t):
    x = refs[0][...]
    k = 1
    if has_delta:
        x = x + refs[k][...].T
        k += 1
    g_ref = refs[k]
    outs = refs[k + 1:]
    y = x * lax.rsqrt(jnp.mean(x * x, axis=-1, keepdims=True) + NORM_EPS)
    y = y * g_ref[...]
    k = 0
    if emit_sum:
        outs[k][...] = x
        k += 1
    if emit_norm_f32:
        outs[k][...] = y
    else:
        outs[k][...] = y.astype(BF16)
    if emit_norm_t:
        outs[k + 1][...] = y.T.astype(BF16)


def _addnorm(x, gain, *, delta_t=None, emit_sum=False, emit_norm_f32=False, emit_norm_t=False,
             rows=256):
    t, d = x.shape
    rows = min(rows, t)
    spec = pl.BlockSpec((rows, d), lambda i: (i, 0))
    spec_t = pl.BlockSpec((d, rows), lambda i: (0, i))
    args, in_specs = [x], [spec]
    if delta_t is not None:
        args.append(delta_t)
        in_specs.append(spec_t)
    args.append(gain.reshape(1, d).astype(F32))
    in_specs.append(pl.BlockSpec((1, d), lambda i: (0, 0)))
    out_shapes, out_specs = [], []
    if emit_sum:
        out_shapes.append(jax.ShapeDtypeStruct((t, d), F32))
        out_specs.append(spec)
    out_shapes.append(jax.ShapeDtypeStruct((t, d), F32 if emit_norm_f32 else BF16))
    out_specs.append(spec)
    if emit_norm_t:
        out_shapes.append(jax.ShapeDtypeStruct((d, t), BF16))
        out_specs.append(spec_t)
    return pl.pallas_call(
        functools.partial(_addnorm_kernel, has_delta=delta_t is not None, emit_sum=emit_sum,
                          emit_norm_f32=emit_norm_f32, emit_norm_t=emit_norm_t),
        grid=(t // rows,), in_specs=in_specs, out_specs=out_specs, out_shape=out_shapes,
        compiler_params=_params("parallel"), name="addnorm",
    )(*args)


def _mm_kernel(x_ref, w_ref, *rest, epilogue):
    o_ref = rest[-1]
    acc = jnp.dot(x_ref[...], w_ref[...], preferred_element_type=F32)
    if epilogue == "sigmoid":
        acc = jax.nn.sigmoid(acc)
    elif epilogue == "residual":
        acc = acc + rest[0][...]
    o_ref[...] = acc.astype(o_ref.dtype)


def _matmul(x, w, *, out_dtype, col0=0, n=None, epilogue=None, residual=None, seq=None,
            tm=1024, tn=1024):
    t, k = x.shape
    n = w.shape[1] if n is None else n
    tm, tn = min(tm, t), min(tn, n)
    assert col0 % tn == 0 and n % tn == 0
    cb = col0 // tn
    if seq is None:
        out_shape, out_spec = (t, n), pl.BlockSpec((tm, tn), lambda i, j: (i, j))
    else:
        tm = min(tm, seq)
        per = seq // tm
        out_shape = (t // seq, seq, n)
        out_spec = pl.BlockSpec((None, tm, tn), lambda i, j: (i // per, i % per, j))
    in_specs = [pl.BlockSpec((tm, k), lambda i, j: (i, 0)),
                pl.BlockSpec((k, tn), lambda i, j: (0, j + cb))]
    args = [x, w]
    if epilogue == "residual":
        in_specs.append(pl.BlockSpec((tm, tn), lambda i, j: (i, j)))
        args.append(residual)
    return pl.pallas_call(
        functools.partial(_mm_kernel, epilogue=epilogue),
        grid=(t // tm, n // tn), in_specs=in_specs, out_specs=out_spec,
        out_shape=jax.ShapeDtypeStruct(out_shape, out_dtype),
        compiler_params=_params("parallel", "parallel"), name="matmul",
    )(*args)


def _glu_kernel(z_ref, wa_ref, wb_ref, o_ref):
    z = z_ref[...]
    a = jnp.dot(z, wa_ref[...], preferred_element_type=F32)
    b = jnp.dot(z, wb_ref[...], preferred_element_type=F32)
    o_ref[...] = (a * jax.nn.sigmoid(b)).astype(o_ref.dtype)


def _glu(z, w_glu, *, tm=1024, tn=512):
    bsz, seq, k = z.shape
    t = bsz * seq
    n = w_glu.shape[1] // 2
    tm, tn = min(tm, seq), min(tn, n)
    per = seq // tm
    nb = n // tn
    return pl.pallas_call(
        _glu_kernel, grid=(t // tm, nb),
        in_specs=[pl.BlockSpec((None, tm, k), lambda i, j: (i // per, i % per, 0)),
                  pl.BlockSpec((k, tn), lambda i, j: (0, j)),
                  pl.BlockSpec((k, tn), lambda i, j: (0, j + nb))],
        out_specs=pl.BlockSpec((tm, tn), lambda i, j: (i, j)),
        out_shape=jax.ShapeDtypeStruct((t, n), BF16),
        compiler_params=_params("parallel", "parallel"), name="glu",
    )(z, w_glu, w_glu)


def _merge_kernel(a_ref, s_ref, ga_ref, gs_ref, wa_ref, ws_ref, o_ref):
    a = jnp.dot(a_ref[...], wa_ref[...], preferred_element_type=F32)
    s = jnp.dot(s_ref[...], ws_ref[...], preferred_element_type=F32)
    o_ref[...] = (ga_ref[...].astype(F32) * a + gs_ref[...].astype(F32) * s).astype(o_ref.dtype)


def _merge(attn, ssm, gates, w_ab, w_sb, *, tm=1024, tn=1024):
    t = attn.shape[0]
    n = w_ab.shape[1]
    tm, tn = min(tm, t), min(tn, n)
    nb = n // tn
    return pl.pallas_call(
        _merge_kernel, grid=(t // tm, nb),
        in_specs=[pl.BlockSpec((tm, attn.shape[1]), lambda i, j: (i, 0)),
                  pl.BlockSpec((tm, ssm.shape[1]), lambda i, j: (i, 0)),
                  pl.BlockSpec((tm, tn), lambda i, j: (i, j)),
                  pl.BlockSpec((tm, tn), lambda i, j: (i, j + nb)),
                  pl.BlockSpec((w_ab.shape[0], tn), lambda i, j: (0, j)),
                  pl.BlockSpec((w_sb.shape[0], tn), lambda i, j: (0, j))],
        out_specs=pl.BlockSpec((tm, tn), lambda i, j: (i, j)),
        out_shape=jax.ShapeDtypeStruct((t, n), BF16),
        compiler_params=_params("parallel", "parallel"), name="merge",
    )(attn, ssm, gates, gates, w_ab, w_sb)


def _alibi_slopes():
    return [float(v) for v in
            np.asarray(2.0 ** (-8.0 * np.arange(1, N_Q_HEADS + 1) / N_Q_HEADS), dtype=np.float32)]


def _attn_kernel(sink_ref, q0_ref, q1_ref, q2_ref, q3_ref, kc_ref, kp_ref, vc_ref, vp_ref, o_ref):
    n = pl.program_id(1)
    qi = lax.broadcasted_iota(jnp.int32, (BLOCK, 2 * BLOCK), 0)
    sj = lax.broadcasted_iota(jnp.int32, (BLOCK, 2 * BLOCK), 1)
    dist = qi - sj + BLOCK
    valid = (dist >= 0) & (dist < WINDOW) & ((n > 0) | (sj >= BLOCK))
    distf = dist.astype(F32)
    slopes = _alibi_slopes()
    scale = HEAD_DIM ** -0.5
    for k, q_ref in enumerate((q0_ref, q1_ref, q2_ref, q3_ref)):
        ksl = slice(k * HEAD_DIM, (k + 1) * HEAD_DIM)
        kw = jnp.concatenate([kp_ref[:, ksl], kc_ref[:, ksl]], axis=0)
        vw = jnp.concatenate([vp_ref[:, ksl], vc_ref[:, ksl]], axis=0)
        q4 = jnp.concatenate([q_ref[:, g * HEAD_DIM:(g + 1) * HEAD_DIM]
                              for g in range(Q_PER_KV)], axis=0)
        logits = lax.dot_general(q4, kw, (((1,), (1,)), ((), ())),
                                 preferred_element_type=F32) * scale
        ps, dens = [], []
        for g in range(Q_PER_KV):
            h = k * Q_PER_KV + g
            lg = logits[g * BLOCK:(g + 1) * BLOCK]
            lg = jnp.where(valid, lg - slopes[h] * distf, NEG_INF)
            sink = sink_ref[h]
            m = jnp.maximum(jnp.max(lg, axis=-1, keepdims=True), sink)
            p = jnp.exp(lg - m)
            dens.append(jnp.sum(p, axis=-1, keepdims=True) + jnp.exp(sink - m))
            ps.append(p.astype(BF16))
        pv = jnp.dot(jnp.concatenate(ps, axis=0), vw, preferred_element_type=F32)
        for g in range(Q_PER_KV):
            h = k * Q_PER_KV + g
            o_ref[:, h * HEAD_DIM:(h + 1) * HEAD_DIM] = (
                pv[g * BLOCK:(g + 1) * BLOCK] / dens[g]).astype(o_ref.dtype)


def _attention(qkv, sinks, *, bsz, seq):
    aw = N_Q_HEADS * HEAD_DIM
    kvw = N_KV_HEADS * HEAD_DIM
    assert aw == N_KV_HEADS * kvw
    nb = seq // BLOCK
    koff, voff = aw // kvw, aw // kvw + 1
    q_specs = [pl.BlockSpec((BLOCK, kvw), lambda b, n, g=g: (b * nb + n, g))
               for g in range(N_KV_HEADS)]
    prev = lambda b, n: b * nb + jnp.maximum(n - 1, 0)
    return pl.pallas_call(
        _attn_kernel, grid=(bsz, nb),
        in_specs=[pl.BlockSpec(memory_space=pltpu.SMEM)] + q_specs + [
            pl.BlockSpec((BLOCK, kvw), lambda b, n: (b * nb + n, koff)),
            pl.BlockSpec((BLOCK, kvw), lambda b, n: (prev(b, n), koff)),
            pl.BlockSpec((BLOCK, kvw), lambda b, n: (b * nb + n, voff)),
            pl.BlockSpec((BLOCK, kvw), lambda b, n: (prev(b, n), voff))],
        out_specs=pl.BlockSpec((BLOCK, aw), lambda b, n: (b * nb + n, 0)),
        out_shape=jax.ShapeDtypeStruct((bsz * seq, aw), BF16),
        compiler_params=_params("parallel", "parallel"), name="swa_attention",
    )(sinks.astype(F32), *([qkv] * (N_KV_HEADS + 4)))


S5_SLICE_GROUPS = 16
S5_SLICE_IN = S5_SLICE_GROUPS * SSM_GROUP
S5_SLICE_STATE = S5_SLICE_GROUPS * SSM_STATE


def _s5_kernel(u_ref, bd_ref, lam_ref, cd_ref, d_ref, z_ref, bu_ref, st_ref, tb_ref, *, steps):
    ns = S5_SLICE_STATE
    n_slab = S5_SLICE_IN // LANES

    @pl.when(pl.program_id(1) == 0)
    def _():
        st_ref[...] = jnp.zeros_like(st_ref)

    for b in range(SUBLANES):
        for sl in range(n_slab):
            tb_ref[sl, pl.ds(b, steps, stride=SUBLANES), :] = u_ref[b, :, sl * LANES:(sl + 1) * LANES]
    u = jnp.concatenate([tb_ref[sl] for sl in range(n_slab)], axis=-1)
    bu_ref[...] = jnp.dot(u.astype(BF16), bd_ref[0], preferred_element_type=F32)
    lr = jnp.broadcast_to(lam_ref[0, 0:1, :], (SUBLANES, ns))
    li = jnp.broadcast_to(lam_ref[0, 1:2, :], (SUBLANES, ns))

    def step(t, carry):
        xr, xi = carry
        row = pl.multiple_of(t * SUBLANES, SUBLANES)
        nr = lr * xr - li * xi + bu_ref[pl.ds(row, SUBLANES), 0:ns]
        ni = lr * xi + li * xr + bu_ref[pl.ds(row, SUBLANES), ns:2 * ns]
        bu_ref[pl.ds(row, SUBLANES), 0:ns] = nr
        bu_ref[pl.ds(row, SUBLANES), ns:2 * ns] = ni
        return nr, ni

    xr, xi = lax.fori_loop(0, steps, step, (st_ref[:, 0:ns], st_ref[:, ns:2 * ns]), unroll=4)
    st_ref[:, 0:ns] = xr
    st_ref[:, ns:2 * ns] = xi
    y = jnp.dot(bu_ref[...].astype(BF16), cd_ref[0], preferred_element_type=F32)
    z = jax.nn.gelu(y + d_ref[0] * u)
    for sl in range(n_slab):
        tb_ref[sl] = z[:, sl * LANES:(sl + 1) * LANES]
    for b in range(SUBLANES):
        for sl in range(n_slab):
            z_ref[b, :, sl * LANES:(sl + 1) * LANES] = (
                tb_ref[sl, pl.ds(b, steps, stride=SUBLANES), :].astype(z_ref.dtype))


def _s5_prepare(a_re, a_im, log_step, b_re, b_im, c_re, c_im):
    g = a_re.shape[0]
    ns = g // S5_SLICE_GROUPS
    ar, ai = a_re.astype(F32), a_im.astype(F32)
    dt = jnp.exp(log_step.astype(F32))[:, None]
    mag = jnp.exp(ar * dt)
    lr = mag * jnp.cos(ai * dt)
    li = mag * jnp.sin(ai * dt)
    den = ar * ar + ai * ai
    nr = lr - 1.0
    zr = (nr * ar + li * ai) / den
    zi = (li * ar - nr * ai) / den
    br, bi = b_re.astype(F32), b_im.astype(F32)
    bbar_r = zr[..., None] * br - zi[..., None] * bi
    bbar_i = zr[..., None] * bi + zi[..., None] * br
    eye = jnp.eye(S5_SLICE_GROUPS, dtype=F32)

    def bdiag(m):
        m = m.reshape(ns, S5_SLICE_GROUPS, SSM_STATE, SSM_GROUP)
        return jnp.einsum("ngph,gk->nghkp", m, eye).reshape(ns, S5_SLICE_IN, S5_SLICE_STATE)

    def cdiag(m):
        m = m.reshape(ns, S5_SLICE_GROUPS, SSM_GROUP, SSM_STATE)
        return jnp.einsum("nghp,gk->nkpgh", m, eye).reshape(ns, S5_SLICE_STATE, S5_SLICE_IN)

    bd = jnp.concatenate([bdiag(bbar_r), bdiag(bbar_i)], axis=-1).astype(BF16)
    cd = jnp.concatenate([cdiag(c_re.astype(F32)), -cdiag(c_im.astype(F32))], axis=1).astype(BF16)
    lam = jnp.stack([lr.reshape(ns, S5_SLICE_STATE), li.reshape(ns, S5_SLICE_STATE)], axis=1)
    return bd, lam, cd


def _s5(u, prep, d_skip, *, bsz, seq, steps=128):
    assert u.shape[:2] == (bsz, seq) and bsz == SUBLANES
    bd, lam, cd = prep
    ns = bd.shape[0]
    steps = min(steps, seq)
    rows = steps * bsz
    width = u.shape[2]
    blk = pl.BlockSpec((bsz, steps, S5_SLICE_IN), lambda j, c: (0, c, j))
    return pl.pallas_call(
        functools.partial(_s5_kernel, steps=steps),
        grid=(ns, seq // steps),
        in_specs=[blk,
                  pl.BlockSpec((1, S5_SLICE_IN, 2 * S5_SLICE_STATE), lambda j, c: (j, 0, 0)),
                  pl.BlockSpec((1, 2, S5_SLICE_STATE), lambda j, c: (j, 0, 0)),
                  pl.BlockSpec((1, 2 * S5_SLICE_STATE, S5_SLICE_IN), lambda j, c: (j, 0, 0)),
                  pl.BlockSpec((1, 1, S5_SLICE_IN), lambda j, c: (j, 0, 0))],
        out_specs=blk,
        out_shape=jax.ShapeDtypeStruct((bsz, seq, width), BF16),
        scratch_shapes=[pltpu.VMEM((rows, 2 * S5_SLICE_STATE), F32),
                        pltpu.VMEM((SUBLANES, 2 * S5_SLICE_STATE), F32),
                        pltpu.VMEM((S5_SLICE_IN // LANES, rows, LANES), F32)],
        compiler_params=_params("parallel", "arbitrary"), name="s5_scan",
    )(u, bd, lam, cd, d_skip.astype(F32).reshape(ns, 1, S5_SLICE_IN))


ROUTE_KEEP = PEER_TOPK + 1
ROUTE_ROWS = 24


def _sorting_network(n):
    pairs = []
    p = 1
    while p < n:
        k = p
        while k >= 1:
            for j in range(k % p, n - k, 2 * k):
                for i in range(min(k, n - j - k)):
                    if (i + j) // (2 * p) == (i + j + k) // (2 * p):
                        pairs.append((i + j, i + j + k))
            k //= 2
        p *= 2
    return pairs


def _extract_sorted(s, count):
    n = s.shape[0] // SUBLANES
    v = [s[r * SUBLANES:(r + 1) * SUBLANES] for r in range(n)]
    wires = 1 << (n - 1).bit_length()
    v += [None] * (wires - n)
    for i, j in _sorting_network(wires):
        if v[j] is None:
            continue
        if v[i] is None:
            v[i], v[j] = v[j], None
        else:
            v[i], v[j] = jnp.maximum(v[i], v[j]), jnp.minimum(v[i], v[j])
    v = v[:n]
    sub = lax.broadcasted_iota(jnp.int32, v[0].shape, 0).astype(F32)
    out = []
    for k in range(count):
        m = jnp.max(v[0], axis=0, keepdims=True)
        out.append(m)
        first = jnp.min(jnp.where(v[0] == m, sub, float(SUBLANES)), axis=0, keepdims=True)
        pop = sub == first
        for r in range(min(n, count - k - 1)):
            nxt = v[r + 1] if r + 1 < n else -jnp.inf
            v[r] = jnp.where(pop, nxt, v[r])
    return out


def _route_kernel(q_ref, keys_ref, theta_ref, coef_ref, s2_ref, e2_ref, a_scr, b_scr, *, tt):
    for h in range(PEER_HEADS):
        subs = []
        for c in range(2):
            qh = q_ref[:, (2 * h + c) * N_KEYS:(2 * h + c + 1) * N_KEYS].astype(BF16)
            subs.append(lax.dot_general(keys_ref[h, c], qh, (((1,), (1,)), ((), ())),
                                        preferred_element_type=F32))
        s1, s2 = subs
        a_scr[...] = jnp.full(a_scr.shape, -jnp.inf, F32)
        b_scr[...] = jnp.full(b_scr.shape, -jnp.inf, F32)
        a_rows = _extract_sorted(s1, ROUTE_KEEP)
        b_rows = _extract_sorted(s2, ROUTE_KEEP)
        for r in range(ROUTE_KEEP):
            a_scr[r:r + 1, :] = a_rows[r]
            b_scr[r:r + 1, :] = b_rows[r]
        cand = [a_rows[0] + b_scr[...]]
        for k in range(1, SUBLANES):
            cand.append(a_rows[k] + b_scr[0:SUBLANES, :])
        cand.append(a_scr[SUBLANES:ROUTE_ROWS, :] + b_rows[0])
        top = _extract_sorted(jnp.concatenate(cand, axis=0), ROUTE_KEEP)
        zsum = jnp.zeros((1, tt), F32)
        for r in range(PEER_TOPK):
            zsum = zsum + jnp.exp(top[r] - top[0])
        thr = 0.5 * (top[PEER_TOPK - 1] + top[PEER_TOPK])
        theta_ref[h] = thr - s1
        coef_ref[h] = jnp.exp(s1 - a_rows[0]) / zsum
        s2_ref[h] = s2
        e2_ref[h] = jnp.exp(s2 - b_rows[0])


def _route(qp, keys, *, tt=256):
    t = qp.shape[0]
    tt = min(tt, t)
    shp = jax.ShapeDtypeStruct((PEER_HEADS, N_KEYS, t), F32)
    ospec = pl.BlockSpec((PEER_HEADS, N_KEYS, tt), lambda i: (0, 0, i))
    return pl.pallas_call(
        functools.partial(_route_kernel, tt=tt), grid=(t // tt,),
        in_specs=[pl.BlockSpec((tt, qp.shape[1]), lambda i: (i, 0)),
                  pl.BlockSpec(keys.shape, lambda i: (0, 0, 0, 0))],
        out_specs=[ospec] * 4, out_shape=[shp] * 4,
        scratch_shapes=[pltpu.VMEM((ROUTE_ROWS, tt), F32), pltpu.VMEM((ROUTE_ROWS, tt), F32)],
        compiler_params=_params("parallel"), name="peer_route",
    )(qp, keys)


def _peer_kernel(xt_ref, u_ref, vt_ref, theta_ref, coef_ref, s2_ref, e2_ref, o_ref,
                 ht_scr, ht_next, hg_scr, *, eb, tm, n_blk):
    s = pl.program_id(0)

    @pl.when(s == 0)
    def _():
        ht_scr[...] = jnp.zeros_like(ht_scr)

    @pl.when(lax.rem(jnp.maximum(s - 1, 0), n_blk) == 0)
    def _():
        o_ref[...] = jnp.zeros_like(o_ref)

    n_half = 2
    for c in range(n_half):
        ts = slice(c * (tm // n_half), (c + 1) * (tm // n_half))
        ht_next[:, ts] = jnp.dot(u_ref[...], xt_ref[:, ts], preferred_element_type=F32)
        es = slice(c * (eb // n_half), (c + 1) * (eb // n_half))
        for ii in range(es.start // N_KEYS, es.stop // N_KEYS):
            rs = slice(ii * N_KEYS, (ii + 1) * N_KEYS)
            for lc in range(tm // LANES):
                ls = slice(lc * LANES, (lc + 1) * LANES)
                g = jnp.zeros((N_KEYS, LANES), F32)
                for h in range(PEER_HEADS):
                    th = theta_ref[h, 0, ii:ii + 1, ls]
                    cf = coef_ref[h, 0, ii:ii + 1, ls]
                    g = g + jnp.where(s2_ref[h, :, ls] >= th, e2_ref[h, :, ls] * cf, 0.0)
                hg_scr[rs, ls] = (jax.nn.gelu(ht_scr[rs, ls]) * g).astype(BF16)
        o_ref[...] += jnp.dot(vt_ref[:, es], hg_scr[es, :], preferred_element_type=F32)
    ht_scr[...] = ht_next[...]


def _peer(xnt, u_tab, vt_tab, route, *, tm=512, eb=512):
    d, t = xnt.shape
    n_blk = u_tab.shape[0] // eb
    tm = min(tm, t)
    n_i = eb // N_KEYS
    theta, coef, s2, e2 = route
    theta = theta.reshape(PEER_HEADS, N_KEYS // n_i, n_i, t)
    coef = coef.reshape(PEER_HEADS, N_KEYS // n_i, n_i, t)
    n_pairs = (t // tm) * n_blk
    cur = lambda s: jnp.minimum(s, n_pairs - 1)
    prv = lambda s: jnp.maximum(s - 1, 0)
    tile = lambda p: p // n_blk
    blk = lambda p: p % n_blk
    ispec = pl.BlockSpec((PEER_HEADS, 1, n_i, tm), lambda s: (0, blk(prv(s)), 0, tile(prv(s))))
    rspec = pl.BlockSpec((PEER_HEADS, N_KEYS, tm), lambda s: (0, 0, tile(prv(s))))
    return pl.pallas_call(
        functools.partial(_peer_kernel, eb=eb, tm=tm, n_blk=n_blk), grid=(n_pairs + 1,),
        in_specs=[pl.BlockSpec((d, tm), lambda s: (0, tile(cur(s)))),
                  pl.BlockSpec((eb, d), lambda s: (blk(cur(s)), 0)),
                  pl.BlockSpec((d, eb), lambda s: (0, blk(prv(s)))),
                  ispec, ispec, rspec, rspec],
        out_specs=pl.BlockSpec((d, tm), lambda s: (0, tile(prv(s)))),
        out_shape=jax.ShapeDtypeStruct((d, t), F32),
        scratch_shapes=[pltpu.VMEM((eb, tm), F32), pltpu.VMEM((eb, tm), F32),
                        pltpu.VMEM((eb, tm), BF16)],
        compiler_params=_params("arbitrary"), name="peer_experts",
    )(xnt, u_tab, vt_tab, theta, coef, s2, e2)


def kernel(x, ln_mix, w_in, ssm_a_re, ssm_a_im, ssm_log_step, ssm_b_re, ssm_b_im, ssm_c_re,
           ssm_c_im, ssm_d, w_glu, attn_sinks, w_attn_branch, w_ssm_branch, w_out, ln_ffn,
           w_peer_q, peer_keys, peer_u, peer_v, ln_final):
    bsz, seq, d = x.shape
    depth = w_in.shape[0]
    aw = N_Q_HEADS * HEAD_DIM
    qkv_w = aw + 2 * N_KV_HEADS * HEAD_DIM
    ssm_w = ssm_d.shape[-1]

    h = x.reshape(bsz * seq, d)
    pt = None
    for l in range(depth):
        if l == 0:
            xres = h
            (xn,) = _addnorm(h, ln_mix[l])
        else:
            xres, xn = _addnorm(h, ln_mix[l], delta_t=pt, emit_sum=True)
        w_l = _cast_bf16(w_in, l)
        qkv = _matmul(xn, w_l, n=qkv_w, out_dtype=BF16)
        u = _matmul(xn, w_l, col0=qkv_w, n=ssm_w, out_dtype=F32, seq=seq)
        gates = _matmul(xn, w_l, col0=qkv_w + ssm_w, n=2 * d, out_dtype=BF16, epilogue="sigmoid")
        attn = _attention(qkv, attn_sinks[l], bsz=bsz, seq=seq)
        prep = _s5_prepare(ssm_a_re[l], ssm_a_im[l], ssm_log_step[l], ssm_b_re[l], ssm_b_im[l],
                           ssm_c_re[l], ssm_c_im[l])
        z = _s5(u, prep, ssm_d[l], bsz=bsz, seq=seq)
        ssm = _glu(z, _cast_bf16(w_glu, l))
        merged = _merge(attn, ssm, gates, _cast_bf16(w_attn_branch, l),
                        _cast_bf16(w_ssm_branch, l))
        h = _matmul(merged, _cast_bf16(w_out, l), out_dtype=F32, epilogue="residual",
                    residual=xres)
        hn, hnt = _addnorm(h, ln_ffn[l], emit_norm_t=True)
        qp = _matmul(hn, _cast_bf16(w_peer_q, l), out_dtype=F32)
        route = _route(qp, peer_keys[l].astype(BF16))
        pt = _peer(hnt, _cast_bf16(peer_u, l), _cast_bf16(peer_v, l, transpose=True), route)
    (out,) = _addnorm(h, ln_final, delta_t=pt, emit_norm_f32=True)
    return out.reshape(bsz, seq, d)
```

```python
import functools

import numpy as np
import jax
import jax.numpy as jnp
from jax import lax
from jax.experimental import pallas as pl
from jax.experimental.pallas import tpu as pltpu

F32 = jnp.float32
BF16 = jnp.bfloat16

HEAD_DIM = 128
N_Q_HEADS = 16
N_KV_HEADS = 4
Q_PER_KV = 4
WINDOW = 128
BLOCK = 128
NEG_INF = -1e30
SSM_GROUP = 16
SSM_STATE = 64
PEER_HEADS = 8
N_KEYS = 128
PEER_TOPK = 16
NORM_EPS = 1e-6

V7X_VMEM_BYTES = 64 * 1024 * 1024
VMEM_LIMIT = V7X_VMEM_BYTES - 8 * 1024 * 1024
V7X_MXU_DIM = 256
SUBLANES = 8
LANES = 128


def _params(*sem, flags=None):
    return pltpu.CompilerParams(dimension_semantics=sem, vmem_limit_bytes=VMEM_LIMIT, flags=flags)


CAST_BLOCK_ELEMS = 2 * 1024 * 1024


def _cast_kernel(w_ref, o_ref, *, transpose):
    w = w_ref[...]
    o_ref[...] = (w.T if transpose else w).astype(o_ref.dtype)


def _cast_bf16(w, l, *, transpose=False):
    _, r, c = w.shape
    rows = r
    while rows * c > CAST_BLOCK_ELEMS and rows % 2 == 0 and rows > LANES:
        rows //= 2
    out_shape, out_spec = ((c, r), pl.BlockSpec((c, rows), lambda i: (0, i))) if transpose else (
        (r, c), pl.BlockSpec((rows, c), lambda i: (i, 0)))
    return pl.pallas_call(
        functools.partial(_cast_kernel, transpose=transpose), grid=(r // rows,),
        in_specs=[pl.BlockSpec((None, rows, c), lambda i: (l, i, 0))],
        out_specs=out_spec, out_shape=jax.ShapeDtypeStruct(out_shape, BF16),
        compiler_params=_params("parallel"), name="cast_bf16",
    )(w)


def _addnorm_kernel(*refs, has_delta, emit_sum, emit_norm_f32, emit_norm_t):
    x = refs[0][...]
    k = 1
    if has_delta:
        x = x + refs[k][...].T
        k += 1
    g_ref = refs[k]
    outs = refs[k + 1:]
    y = x * lax.rsqrt(jnp.mean(x * x, axis=-1, keepdims=True) + NORM_EPS)
    y = y * g_ref[...]
    k = 0
    if emit_sum:
        outs[k][...] = x
        k += 1
    if emit_norm_f32:
        outs[k][...] = y
    else:
        outs[k][...] = y.astype(BF16)
    if emit_norm_t:
        outs[k + 1][...] = y.T.astype(BF16)


def _addnorm(x, gain, *, delta_t=None, emit_sum=False, emit_norm_f32=False, emit_norm_t=False,
             rows=256):
    t, d = x.shape
    rows = min(rows, t)
    spec = pl.BlockSpec((rows, d), lambda i: (i, 0))
    spec_t = pl.BlockSpec((d, rows), lambda i: (0, i))
    args, in_specs = [x], [spec]
    if delta_t is not None:
        args.append(delta_t)
        in_specs.append(spec_t)
    args.append(gain.reshape(1, d).astype(F32))
    in_specs.append(pl.BlockSpec((1, d), lambda i: (0, 0)))
    out_shapes, out_specs = [], []
    if emit_sum:
        out_shapes.append(jax.ShapeDtypeStruct((t, d), F32))
        out_specs.append(spec)
    out_shapes.append(jax.ShapeDtypeStruct((t, d), F32 if emit_norm_f32 else BF16))
    out_specs.append(spec)
    if emit_norm_t:
        out_shapes.append(jax.ShapeDtypeStruct((d, t), BF16))
        out_specs.append(spec_t)
    return pl.pallas_call(
        functools.partial(_addnorm_kernel, has_delta=delta_t is not None, emit_sum=emit_sum,
                          emit_norm_f32=emit_norm_f32, emit_norm_t=emit_norm_t),
        grid=(t // rows,), in_specs=in_specs, out_specs=out_specs, out_shape=out_shapes,
        compiler_params=_params("parallel"), name="addnorm",
    )(*args)


def _mm_kernel(x_ref, w_ref, *rest, epilogue):
    o_ref = rest[-1]
    acc = jnp.dot(x_ref[...], w_ref[...], preferred_element_type=F32)
    if epilogue == "sigmoid":
        acc = jax.nn.sigmoid(acc)
    elif epilogue == "residual":
        acc = acc + rest[0][...]
    o_ref[...] = acc.astype(o_ref.dtype)


def _matmul(x, w, *, out_dtype, col0=0, n=None, epilogue=None, residual=None, seq=None,
            tm=1024, tn=1024):
    t, k = x.shape
    n = w.shape[1] if n is None else n
    tm, tn = min(tm, t), min(tn, n)
    assert col0 % tn == 0 and n % tn == 0
    cb = col0 // tn
    if seq is None:
        out_shape, out_spec = (t, n), pl.BlockSpec((tm, tn), lambda i, j: (i, j))
    else:
        tm = min(tm, seq)
        per = seq // tm
        out_shape = (t // seq, seq, n)
        out_spec = pl.BlockSpec((None, tm, tn), lambda i, j: (i // per, i % per, j))
    in_specs = [pl.BlockSpec((tm, k), lambda i, j: (i, 0)),
                pl.BlockSpec((k, tn), lambda i, j: (0, j + cb))]
    args = [x, w]
    if epilogue == "residual":
        in_specs.append(pl.BlockSpec((tm, tn), lambda i, j: (i, j)))
        args.append(residual)
    return pl.pallas_call(
        functools.partial(_mm_kernel, epilogue=epilogue),
        grid=(t // tm, n // tn), in_specs=in_specs, out_specs=out_spec,
        out_shape=jax.ShapeDtypeStruct(out_shape, out_dtype),
        compiler_params=_params("parallel", "parallel"), name="matmul",
    )(*args)


def _glu_kernel(z_ref, wa_ref, wb_ref, o_ref):
    z = z_ref[...]
    a = jnp.dot(z, wa_ref[...], preferred_element_type=F32)
    b = jnp.dot(z, wb_ref[...], preferred_element_type=F32)
    o_ref[...] = (a * jax.nn.sigmoid(b)).astype(o_ref.dtype)


def _glu(z, w_glu, *, tm=1024, tn=512):
    bsz, seq, k = z.shape
    t = bsz * seq
    n = w_glu.shape[1] // 2
    tm, tn = min(tm, seq), min(tn, n)
    per = seq // tm
    nb = n // tn
    return pl.pallas_call(
        _glu_kernel, grid=(t // tm, nb),
        in_specs=[pl.BlockSpec((None, tm, k), lambda i, j: (i // per, i % per, 0)),
                  pl.BlockSpec((k, tn), lambda i, j: (0, j)),
                  pl.BlockSpec((k, tn), lambda i, j: (0, j + nb))],
        out_specs=pl.BlockSpec((tm, tn), lambda i, j: (i, j)),
        out_shape=jax.ShapeDtypeStruct((t, n), BF16),
        compiler_params=_params("parallel", "parallel"), name="glu",
    )(z, w_glu, w_glu)


def _merge_kernel(a_ref, s_ref, ga_ref, gs_ref, wa_ref, ws_ref, o_ref):
    a = jnp.dot(a_ref[...], wa_ref[...], preferred_element_type=F32)
    s = jnp.dot(s_ref[...], ws_ref[...], preferred_element_type=F32)
    o_ref[...] = (ga_ref[...].astype(F32) * a + gs_ref[...].astype(F32) * s).astype(o_ref.dtype)


def _merge(attn, ssm, gates, w_ab, w_sb, *, tm=1024, tn=1024):
    t = attn.shape[0]
    n = w_ab.shape[1]
    tm, tn = min(tm, t), min(tn, n)
    nb = n // tn
    return pl.pallas_call(
        _merge_kernel, grid=(t // tm, nb),
        in_specs=[pl.BlockSpec((tm, attn.shape[1]), lambda i, j: (i, 0)),
                  pl.BlockSpec((tm, ssm.shape[1]), lambda i, j: (i, 0)),
                  pl.BlockSpec((tm, tn), lambda i, j: (i, j)),
                  pl.BlockSpec((tm, tn), lambda i, j: (i, j + nb)),
                  pl.BlockSpec((w_ab.shape[0], tn), lambda i, j: (0, j)),
                  pl.BlockSpec((w_sb.shape[0], tn), lambda i, j: (0, j))],
        out_specs=pl.BlockSpec((tm, tn), lambda i, j: (i, j)),
        out_shape=jax.ShapeDtypeStruct((t, n), BF16),
        compiler_params=_params("parallel", "parallel"), name="merge",
    )(attn, ssm, gates, gates, w_ab, w_sb)


def _alibi_slopes():
    return [float(v) for v in
            np.asarray(2.0 ** (-8.0 * np.arange(1, N_Q_HEADS + 1) / N_Q_HEADS), dtype=np.float32)]


def _attn_kernel(sink_ref, q0_ref, q1_ref, q2_ref, q3_ref, kc_ref, kp_ref, vc_ref, vp_ref, o_ref):
    n = pl.program_id(1)
    qi = lax.broadcasted_iota(jnp.int32, (BLOCK, 2 * BLOCK), 0)
    sj = lax.broadcasted_iota(jnp.int32, (BLOCK, 2 * BLOCK), 1)
    dist = qi - sj + BLOCK
    valid = (dist >= 0) & (dist < WINDOW) & ((n > 0) | (sj >= BLOCK))
    distf = dist.astype(F32)
    slopes = _alibi_slopes()
    scale = HEAD_DIM ** -0.5
    for k, q_ref in enumerate((q0_ref, q1_ref, q2_ref, q3_ref)):
        ksl = slice(k * HEAD_DIM, (k + 1) * HEAD_DIM)
        kw = jnp.concatenate([kp_ref[:, ksl], kc_ref[:, ksl]], axis=0)
        vw = jnp.concatenate([vp_ref[:, ksl], vc_ref[:, ksl]], axis=0)
        q4 = jnp.concatenate([q_ref[:, g * HEAD_DIM:(g + 1) * HEAD_DIM]
                              for g in range(Q_PER_KV)], axis=0)
        logits = lax.dot_general(q4, kw, (((1,), (1,)), ((), ())),
                                 preferred_element_type=F32) * scale
        ps, dens = [], []
        for g in range(Q_PER_KV):
            h = k * Q_PER_KV + g
            lg = logits[g * BLOCK:(g + 1) * BLOCK]
            lg = jnp.where(valid, lg - slopes[h] * distf, NEG_INF)
            sink = sink_ref[h]
            m = jnp.maximum(jnp.max(lg, axis=-1, keepdims=True), sink)
            p = jnp.exp(lg - m)
            dens.append(jnp.sum(p, axis=-1, keepdims=True) + jnp.exp(sink - m))
            ps.append(p.astype(BF16))
        pv = jnp.dot(jnp.concatenate(ps, axis=0), vw, preferred_element_type=F32)
        for g in range(Q_PER_KV):
            h = k * Q_PER_KV + g
            o_ref[:, h * HEAD_DIM:(h + 1) * HEAD_DIM] = (
                pv[g * BLOCK:(g + 1) * BLOCK] / dens[g]).astype(o_ref.dtype)


def _attention(qkv, sinks, *, bsz, seq):
    aw = N_Q_HEADS * HEAD_DIM
    kvw = N_KV_HEADS * HEAD_DIM
    assert aw == N_KV_HEADS * kvw
    nb = seq // BLOCK
    koff, voff = aw // kvw, aw // kvw + 1
    q_specs = [pl.BlockSpec((BLOCK, kvw), lambda b, n, g=g: (b * nb + n, g))
               for g in range(N_KV_HEADS)]
    prev = lambda b, n: b * nb + jnp.maximum(n - 1, 0)
    return pl.pallas_call(
        _attn_kernel, grid=(bsz, nb),
        in_specs=[pl.BlockSpec(memory_space=pltpu.SMEM)] + q_specs + [
            pl.BlockSpec((BLOCK, kvw), lambda b, n: (b * nb + n, koff)),
            pl.BlockSpec((BLOCK, kvw), lambda b, n: (prev(b, n), koff)),
            pl.BlockSpec((BLOCK, kvw), lambda b, n: (b * nb + n, voff)),
            pl.BlockSpec((BLOCK, kvw), lambda b, n: (prev(b, n), voff))],
        out_specs=pl.BlockSpec((BLOCK, aw), lambda b, n: (b * nb + n, 0)),
        out_shape=jax.ShapeDtypeStruct((bsz * seq, aw), BF16),
        compiler_params=_params("parallel", "parallel"), name="swa_attention",
    )(sinks.astype(F32), *([qkv] * (N_KV_HEADS + 4)))


S5_SLICE_GROUPS = 16
S5_SLICE_IN = S5_SLICE_GROUPS * SSM_GROUP
S5_SLICE_STATE = S5_SLICE_GROUPS * SSM_STATE


def _s5_kernel(u_ref, bd_ref, lam_ref, cd_ref, d_ref, z_ref, bu_ref, st_ref, tb_ref, *, steps):
    ns = S5_SLICE_STATE
    n_slab = S5_SLICE_IN // LANES

    @pl.when(pl.program_id(1) == 0)
    def _():
        st_ref[...] = jnp.zeros_like(st_ref)

    for b in range(SUBLANES):
        for sl in range(n_slab):
            tb_ref[sl, pl.ds(b, steps, stride=SUBLANES), :] = u_ref[b, :, sl * LANES:(sl + 1) * LANES]
    u = jnp.concatenate([tb_ref[sl] for sl in range(n_slab)], axis=-1)
    bu_ref[...] = jnp.dot(u.astype(BF16), bd_ref[0], preferred_element_type=F32)
    lr = jnp.broadcast_to(lam_ref[0, 0:1, :], (SUBLANES, ns))
    li = jnp.broadcast_to(lam_ref[0, 1:2, :], (SUBLANES, ns))

    def step(t, carry):
        xr, xi = carry
        row = pl.multiple_of(t * SUBLANES, SUBLANES)
        nr = lr * xr - li * xi + bu_ref[pl.ds(row, SUBLANES), 0:ns]
        ni = lr * xi + li * xr + bu_ref[pl.ds(row, SUBLANES), ns:2 * ns]
        bu_ref[pl.ds(row, SUBLANES), 0:ns] = nr
        bu_ref[pl.ds(row, SUBLANES), ns:2 * ns] = ni
        return nr, ni

    xr, xi = lax.fori_loop(0, steps, step, (st_ref[:, 0:ns], st_ref[:, ns:2 * ns]), unroll=4)
    st_ref[:, 0:ns] = xr
    st_ref[:, ns:2 * ns] = xi
    y = jnp.dot(bu_ref[...].astype(BF16), cd_ref[0], preferred_element_type=F32)
    z = jax.nn.gelu(y + d_ref[0] * u)
    for sl in range(n_slab):
        tb_ref[sl] = z[:, sl * LANES:(sl + 1) * LANES]
    for b in range(SUBLANES):
        for sl in range(n_slab):
            z_ref[b, :, sl * LANES:(sl + 1) * LANES] = (
                tb_ref[sl, pl.ds(b, steps, stride=SUBLANES), :].astype(z_ref.dtype))


def _s5_prepare(a_re, a_im, log_step, b_re, b_im, c_re, c_im):
    g = a_re.shape[0]
    ns = g // S5_SLICE_GROUPS
    ar, ai = a_re.astype(F32), a_im.astype(F32)
    dt = jnp.exp(log_step.astype(F32))[:, None]
    mag = jnp.exp(ar * dt)
    lr = mag * jnp.cos(ai * dt)
    li = mag * jnp.sin(ai * dt)
    den = ar * ar + ai * ai
    nr = lr - 1.0
    zr = (nr * ar + li * ai) / den
    zi = (li * ar - nr * ai) / den
    br, bi = b_re.astype(F32), b_im.astype(F32)
    bbar_r = zr[..., None] * br - zi[..., None] * bi
    bbar_i = zr[..., None] * bi + zi[..., None] * br
    eye = jnp.eye(S5_SLICE_GROUPS, dtype=F32)

    def bdiag(m):
        m = m.reshape(ns, S5_SLICE_GROUPS, SSM_STATE, SSM_GROUP)
        return jnp.einsum("ngph,gk->nghkp", m, eye).reshape(ns, S5_SLICE_IN, S5_SLICE_STATE)

    def cdiag(m):
        m = m.reshape(ns, S5_SLICE_GROUPS, SSM_GROUP, SSM_STATE)
        return jnp.einsum("nghp,gk->nkpgh", m, eye).reshape(ns, S5_SLICE_STATE, S5_SLICE_IN)

    bd = jnp.concatenate([bdiag(bbar_r), bdiag(bbar_i)], axis=-1).astype(BF16)
    cd = jnp.concatenate([cdiag(c_re.astype(F32)), -cdiag(c_im.astype(F32))], axis=1).astype(BF16)
    lam = jnp.stack([lr.reshape(ns, S5_SLICE_STATE), li.reshape(ns, S5_SLICE_STATE)], axis=1)
    return bd, lam, cd


def _s5(u, prep, d_skip, *, bsz, seq, steps=128):
    assert u.shape[:2] == (bsz, seq) and bsz == SUBLANES
    bd, lam, cd = prep
    ns = bd.shape[0]
    steps = min(steps, seq)
    rows = steps * bsz
    width = u.shape[2]
    blk = pl.BlockSpec((bsz, steps, S5_SLICE_IN), lambda j, c: (0, c, j))
    return pl.pallas_call(
        functools.partial(_s5_kernel, steps=steps),
        grid=(ns, seq // steps),
        in_specs=[blk,
                  pl.BlockSpec((1, S5_SLICE_IN, 2 * S5_SLICE_STATE), lambda j, c: (j, 0, 0)),
                  pl.BlockSpec((1, 2, S5_SLICE_STATE), lambda j, c: (j, 0, 0)),
                  pl.BlockSpec((1, 2 * S5_SLICE_STATE, S5_SLICE_IN), lambda j, c: (j, 0, 0)),
                  pl.BlockSpec((1, 1, S5_SLICE_IN), lambda j, c: (j, 0, 0))],
        out_specs=blk,
        out_shape=jax.ShapeDtypeStruct((bsz, seq, width), BF16),
        scratch_shapes=[pltpu.VMEM((rows, 2 * S5_SLICE_STATE), F32),
                        pltpu.VMEM((SUBLANES, 2 * S5_SLICE_STATE), F32),
                        pltpu.VMEM((S5_SLICE_IN // LANES, rows, LANES), F32)],
        compiler_params=_params("parallel", "arbitrary"), name="s5_scan",
    )(u, bd, lam, cd, d_skip.astype(F32).reshape(ns, 1, S5_SLICE_IN))


ROUTE_KEEP = PEER_TOPK + 1
ROUTE_ROWS = 24
PEER_BLOCK = 512


def _sorting_network(n):
    pairs = []
    p = 1
    while p < n:
        k = p
        while k >= 1:
            for j in range(k % p, n - k, 2 * k):
                for i in range(min(k, n - j - k)):
                    if (i + j) // (2 * p) == (i + j + k) // (2 * p):
                        pairs.append((i + j, i + j + k))
            k //= 2
        p *= 2
    return pairs


def _extract_sorted(s, count):
    n = s.shape[0] // SUBLANES
    v = [s[r * SUBLANES:(r + 1) * SUBLANES] for r in range(n)]
    wires = 1 << (n - 1).bit_length()
    v += [None] * (wires - n)
    for i, j in _sorting_network(wires):
        if v[j] is None:
            continue
        if v[i] is None:
            v[i], v[j] = v[j], None
        else:
            v[i], v[j] = jnp.maximum(v[i], v[j]), jnp.minimum(v[i], v[j])
    v = v[:n]
    sub = lax.broadcasted_iota(jnp.int32, v[0].shape, 0).astype(F32)
    out = []
    for k in range(count):
        m = jnp.max(v[0], axis=0, keepdims=True)
        out.append(m)
        first = jnp.min(jnp.where(v[0] == m, sub, float(SUBLANES)), axis=0, keepdims=True)
        pop = sub == first
        for r in range(min(n, count - k - 1)):
            nxt = v[r + 1] if r + 1 < n else -jnp.inf
            v[r] = jnp.where(pop, nxt, v[r])
    return out


def _route_kernel(q_ref, keys_ref, theta_ref, coef_ref, s2_ref, e2_ref, a_scr, b_scr, *, tt,
                  rows_per_block):
    for h in range(PEER_HEADS):
        subs = []
        for c in range(2):
            qh = q_ref[:, (2 * h + c) * N_KEYS:(2 * h + c + 1) * N_KEYS].astype(BF16)
            subs.append(lax.dot_general(keys_ref[h, c], qh, (((1,), (1,)), ((), ())),
                                        preferred_element_type=F32))
        s1, s2 = subs
        a_scr[...] = jnp.full(a_scr.shape, -jnp.inf, F32)
        b_scr[...] = jnp.full(b_scr.shape, -jnp.inf, F32)
        a_rows = _extract_sorted(s1, ROUTE_KEEP)
        b_rows = _extract_sorted(s2, ROUTE_KEEP)
        for r in range(ROUTE_KEEP):
            a_scr[r:r + 1, :] = a_rows[r]
            b_scr[r:r + 1, :] = b_rows[r]
        cand = [a_rows[0] + b_scr[...]]
        for k in range(1, SUBLANES):
            cand.append(a_rows[k] + b_scr[0:SUBLANES, :])
        cand.append(a_scr[SUBLANES:ROUTE_ROWS, :] + b_rows[0])
        top = _extract_sorted(jnp.concatenate(cand, axis=0), ROUTE_KEEP)
        zsum = jnp.zeros((1, tt), F32)
        for r in range(PEER_TOPK):
            zsum = zsum + jnp.exp(top[r] - top[0])
        thr = 0.5 * (top[PEER_TOPK - 1] + top[PEER_TOPK])
        theta = thr - s1
        coef = jnp.exp(s1 - a_rows[0]) / zsum
        for g in range(N_KEYS // rows_per_block):
            rs = slice(g * rows_per_block, (g + 1) * rows_per_block)
            theta_ref[h, g] = theta[rs]
            coef_ref[h, g] = coef[rs]
        s2_ref[h] = s2
        e2_ref[h] = jnp.exp(s2 - b_rows[0])


def _route(qp, keys, *, rows_per_block, tt=256):
    t = qp.shape[0]
    tt = min(tt, t)
    n_grp = N_KEYS // rows_per_block
    shp = jax.ShapeDtypeStruct((PEER_HEADS, N_KEYS, t), F32)
    ospec = pl.BlockSpec((PEER_HEADS, N_KEYS, tt), lambda i: (0, 0, i))
    shp_i = jax.ShapeDtypeStruct((PEER_HEADS, n_grp, rows_per_block, t), F32)
    ospec_i = pl.BlockSpec((PEER_HEADS, n_grp, rows_per_block, tt), lambda i: (0, 0, 0, i))
    return pl.pallas_call(
        functools.partial(_route_kernel, tt=tt, rows_per_block=rows_per_block),
        grid=(t // tt,),
        in_specs=[pl.BlockSpec((tt, qp.shape[1]), lambda i: (i, 0)),
                  pl.BlockSpec(keys.shape, lambda i: (0, 0, 0, 0))],
        out_specs=[ospec_i, ospec_i, ospec, ospec], out_shape=[shp_i, shp_i, shp, shp],
        scratch_shapes=[pltpu.VMEM((ROUTE_ROWS, tt), F32), pltpu.VMEM((ROUTE_ROWS, tt), F32)],
        compiler_params=_params("parallel"), name="peer_route",
    )(qp, keys)


def _peer_kernel(xt_ref, u_ref, vt_ref, theta_ref, coef_ref, s2_ref, e2_ref, o_ref,
                 ht_scr, ht_next, hg_scr, *, eb, tm, n_blk):
    s = pl.program_id(0)

    @pl.when(s == 0)
    def _():
        ht_scr[...] = jnp.zeros_like(ht_scr)

    @pl.when(lax.rem(jnp.maximum(s - 1, 0), n_blk) == 0)
    def _():
        o_ref[...] = jnp.zeros_like(o_ref)

    n_half = 2
    for c in range(n_half):
        ts = slice(c * (tm // n_half), (c + 1) * (tm // n_half))
        ht_next[:, ts] = jnp.dot(u_ref[...], xt_ref[:, ts], preferred_element_type=F32)
        es = slice(c * (eb // n_half), (c + 1) * (eb // n_half))
        for ii in range(es.start // N_KEYS, es.stop // N_KEYS):
            rs = slice(ii * N_KEYS, (ii + 1) * N_KEYS)
            for lc in range(tm // LANES):
                ls = slice(lc * LANES, (lc + 1) * LANES)
                g = jnp.zeros((N_KEYS, LANES), F32)
                for h in range(PEER_HEADS):
                    th = theta_ref[h, 0, ii:ii + 1, ls]
                    cf = coef_ref[h, 0, ii:ii + 1, ls]
                    g = g + jnp.where(s2_ref[h, :, ls] >= th, e2_ref[h, :, ls] * cf, 0.0)
                hg_scr[rs, ls] = (jax.nn.gelu(ht_scr[rs, ls]) * g).astype(BF16)
        o_ref[...] += jnp.dot(vt_ref[:, es], hg_scr[es, :], preferred_element_type=F32)
    ht_scr[...] = ht_next[...]


def _peer(xnt, u_tab, vt_tab, route, *, tm=512, eb=PEER_BLOCK):
    d, t = xnt.shape
    n_blk = u_tab.shape[0] // eb
    tm = min(tm, t)
    n_i = eb // N_KEYS
    theta, coef, s2, e2 = route
    assert theta.shape[1:3] == (N_KEYS // n_i, n_i)
    n_pairs = (t // tm) * n_blk
    cur = lambda s: jnp.minimum(s, n_pairs - 1)
    prv = lambda s: jnp.maximum(s - 1, 0)
    tile = lambda p: p // n_blk
    blk = lambda p: p % n_blk
    ispec = pl.BlockSpec((PEER_HEADS, 1, n_i, tm), lambda s: (0, blk(prv(s)), 0, tile(prv(s))))
    rspec = pl.BlockSpec((PEER_HEADS, N_KEYS, tm), lambda s: (0, 0, tile(prv(s))))
    return pl.pallas_call(
        functools.partial(_peer_kernel, eb=eb, tm=tm, n_blk=n_blk), grid=(n_pairs + 1,),
        in_specs=[pl.BlockSpec((d, tm), lambda s: (0, tile(cur(s)))),
                  pl.BlockSpec((eb, d), lambda s: (blk(cur(s)), 0)),
                  pl.BlockSpec((d, eb), lambda s: (0, blk(prv(s)))),
                  ispec, ispec, rspec, rspec],
        out_specs=pl.BlockSpec((d, tm), lambda s: (0, tile(prv(s)))),
        out_shape=jax.ShapeDtypeStruct((d, t), F32),
        scratch_shapes=[pltpu.VMEM((eb, tm), F32), pltpu.VMEM((eb, tm), F32),
                        pltpu.VMEM((eb, tm), BF16)],
        compiler_params=_params("arbitrary"), name="peer_experts",
    )(xnt, u_tab, vt_tab, theta, coef, s2, e2)


def kernel(x, ln_mix, w_in, ssm_a_re, ssm_a_im, ssm_log_step, ssm_b_re, ssm_b_im, ssm_c_re,
           ssm_c_im, ssm_d, w_glu, attn_sinks, w_attn_branch, w_ssm_branch, w_out, ln_ffn,
           w_peer_q, peer_keys, peer_u, peer_v, ln_final):
    bsz, seq, d = x.shape
    depth = w_in.shape[0]
    aw = N_Q_HEADS * HEAD_DIM
    qkv_w = aw + 2 * N_KV_HEADS * HEAD_DIM
    ssm_w = ssm_d.shape[-1]

    h = x.reshape(bsz * seq, d)
    pt = None
    for l in range(depth):
        if l == 0:
            xres = h
            (xn,) = _addnorm(h, ln_mix[l])
        else:
            xres, xn = _addnorm(h, ln_mix[l], delta_t=pt, emit_sum=True)
        w_l = _cast_bf16(w_in, l)
        qkv = _matmul(xn, w_l, n=qkv_w, out_dtype=BF16)
        u = _matmul(xn, w_l, col0=qkv_w, n=ssm_w, out_dtype=F32, seq=seq)
        gates = _matmul(xn, w_l, col0=qkv_w + ssm_w, n=2 * d, out_dtype=BF16, epilogue="sigmoid")
        attn = _attention(qkv, attn_sinks[l], bsz=bsz, seq=seq)
        prep = _s5_prepare(ssm_a_re[l], ssm_a_im[l], ssm_log_step[l], ssm_b_re[l], ssm_b_im[l],
                           ssm_c_re[l], ssm_c_im[l])
        z = _s5(u, prep, ssm_d[l], bsz=bsz, seq=seq)
        ssm = _glu(z, _cast_bf16(w_glu, l))
        merged = _merge(attn, ssm, gates, _cast_bf16(w_attn_branch, l),
                        _cast_bf16(w_ssm_branch, l))
        h = _matmul(merged, _cast_bf16(w_out, l), out_dtype=F32, epilogue="residual",
                    residual=xres)
        hn, hnt = _addnorm(h, ln_ffn[l], emit_norm_t=True)
        qp = _matmul(hn, _cast_bf16(w_peer_q, l), out_dtype=F32)
        route = _route(qp, peer_keys[l].astype(BF16), rows_per_block=PEER_BLOCK // N_KEYS)
        pt = _peer(hnt, _cast_bf16(peer_u, l), _cast_bf16(peer_v, l, transpose=True), route)
    (out,) = _addnorm(h, ln_final, delta_t=pt, emit_norm_f32=True)
    return out.reshape(bsz, seq, d)
```

```python
import functools

import numpy as np
import jax
import jax.numpy as jnp
from jax import lax
from jax.experimental import pallas as pl
from jax.experimental.pallas import tpu as pltpu

F32 = jnp.float32
BF16 = jnp.bfloat16

HEAD_DIM = 128
N_Q_HEADS = 16
N_KV_HEADS = 4
Q_PER_KV = 4
WINDOW = 128
BLOCK = 128
NEG_INF = -1e30
SSM_GROUP = 16
SSM_STATE = 64
PEER_HEADS = 8
N_KEYS = 128
PEER_TOPK = 16
NORM_EPS = 1e-6

V7X_VMEM_BYTES = 64 * 1024 * 1024
VMEM_LIMIT = V7X_VMEM_BYTES - 8 * 1024 * 1024
V7X_MXU_DIM = 256
SUBLANES = 8
LANES = 128


def _params(*sem, flags=None):
    return pltpu.CompilerParams(dimension_semantics=sem, vmem_limit_bytes=VMEM_LIMIT, flags=flags)


CAST_BLOCK_ELEMS = 2 * 1024 * 1024


def _cast_kernel(w_ref, o_ref, *, transpose):
    w = w_ref[...]
    o_ref[...] = (w.T if transpose else w).astype(o_ref.dtype)


def _cast_bf16(w, l, *, transpose=False):
    _, r, c = w.shape
    rows = r
    while rows * c > CAST_BLOCK_ELEMS and rows % 2 == 0 and rows > LANES:
        rows //= 2
    out_shape, out_spec = ((c, r), pl.BlockSpec((c, rows), lambda i: (0, i))) if transpose else (
        (r, c), pl.BlockSpec((rows, c), lambda i: (i, 0)))
    return pl.pallas_call(
        functools.partial(_cast_kernel, transpose=transpose), grid=(r // rows,),
        in_specs=[pl.BlockSpec((None, rows, c), lambda i: (l, i, 0))],
        out_specs=out_spec, out_shape=jax.ShapeDtypeStruct(out_shape, BF16),
        compiler_params=_params("parallel"), name="cast_bf16",
    )(w)


def _addnorm_kernel(*refs, has_delta, emit_sum, emit_norm_f32, emit_norm_t):
    x = refs[0][...]
    k = 1
    if has_delta:
        x = x + refs[k][...].T
        k += 1
    g_ref = refs[k]
    outs = refs[k + 1:]
    y = x * lax.rsqrt(jnp.mean(x * x, axis=-1, keepdims=True) + NORM_EPS)
    y = y * g_ref[...]
    k = 0
    if emit_sum:
        outs[k][...] = x
        k += 1
    if emit_norm_f32:
        outs[k][...] = y
    else:
        outs[k][...] = y.astype(BF16)
    if emit_norm_t:
        outs[k + 1][...] = y.T.astype(BF16)


def _addnorm(x, gain, *, delta_t=None, emit_sum=False, emit_norm_f32=False, emit_norm_t=False,
             rows=256):
    t, d = x.shape
    rows = min(rows, t)
    spec = pl.BlockSpec((rows, d), lambda i: (i, 0))
    spec_t = pl.BlockSpec((d, rows), lambda i: (0, i))
    args, in_specs = [x], [spec]
    if delta_t is not None:
        args.append(delta_t)
        in_specs.append(spec_t)
    args.append(gain.reshape(1, d).astype(F32))
    in_specs.append(pl.BlockSpec((1, d), lambda i: (0, 0)))
    out_shapes, out_specs = [], []
    if emit_sum:
        out_shapes.append(jax.ShapeDtypeStruct((t, d), F32))
        out_specs.append(spec)
    out_shapes.append(jax.ShapeDtypeStruct((t, d), F32 if emit_norm_f32 else BF16))
    out_specs.append(spec)
    if emit_norm_t:
        out_shapes.append(jax.ShapeDtypeStruct((d, t), BF16))
        out_specs.append(spec_t)
    return pl.pallas_call(
        functools.partial(_addnorm_kernel, has_delta=delta_t is not None, emit_sum=emit_sum,
                          emit_norm_f32=emit_norm_f32, emit_norm_t=emit_norm_t),
        grid=(t // rows,), in_specs=in_specs, out_specs=out_specs, out_shape=out_shapes,
        compiler_params=_params("parallel"), name="addnorm",
    )(*args)


def _mm_kernel(x_ref, w_ref, *rest, epilogue):
    o_ref = rest[-1]
    acc = jnp.dot(x_ref[...], w_ref[...], preferred_element_type=F32)
    if epilogue == "sigmoid":
        acc = jax.nn.sigmoid(acc)
    elif epilogue == "residual":
        acc = acc + rest[0][...]
    o_ref[...] = acc.astype(o_ref.dtype)


def _matmul(x, w, *, out_dtype, col0=0, n=None, epilogue=None, residual=None, seq=None,
            tm=1024, tn=1024):
    t, k = x.shape
    n = w.shape[1] if n is None else n
    tm, tn = min(tm, t), min(tn, n)
    assert col0 % tn == 0 and n % tn == 0
    cb = col0 // tn
    if seq is None:
        out_shape, out_spec = (t, n), pl.BlockSpec((tm, tn), lambda i, j: (i, j))
    else:
        tm = min(tm, seq)
        per = seq // tm
        out_shape = (t // seq, seq, n)
        out_spec = pl.BlockSpec((None, tm, tn), lambda i, j: (i // per, i % per, j))
    in_specs = [pl.BlockSpec((tm, k), lambda i, j: (i, 0)),
                pl.BlockSpec((k, tn), lambda i, j: (0, j + cb))]
    args = [x, w]
    if epilogue == "residual":
        in_specs.append(pl.BlockSpec((tm, tn), lambda i, j: (i, j)))
        args.append(residual)
    return pl.pallas_call(
        functools.partial(_mm_kernel, epilogue=epilogue),
        grid=(t // tm, n // tn), in_specs=in_specs, out_specs=out_spec,
        out_shape=jax.ShapeDtypeStruct(out_shape, out_dtype),
        compiler_params=_params("parallel", "parallel"), name="matmul",
    )(*args)


def _glu_kernel(z_ref, wa_ref, wb_ref, o_ref):
    z = z_ref[...]
    a = jnp.dot(z, wa_ref[...], preferred_element_type=F32)
    b = jnp.dot(z, wb_ref[...], preferred_element_type=F32)
    o_ref[...] = (a * jax.nn.sigmoid(b)).astype(o_ref.dtype)


def _glu(z, w_glu, *, tm=1024, tn=512):
    bsz, seq, k = z.shape
    t = bsz * seq
    n = w_glu.shape[1] // 2
    tm, tn = min(tm, seq), min(tn, n)
    per = seq // tm
    nb = n // tn
    return pl.pallas_call(
        _glu_kernel, grid=(t // tm, nb),
        in_specs=[pl.BlockSpec((None, tm, k), lambda i, j: (i // per, i % per, 0)),
                  pl.BlockSpec((k, tn), lambda i, j: (0, j)),
                  pl.BlockSpec((k, tn), lambda i, j: (0, j + nb))],
        out_specs=pl.BlockSpec((tm, tn), lambda i, j: (i, j)),
        out_shape=jax.ShapeDtypeStruct((t, n), BF16),
        compiler_params=_params("parallel", "parallel"), name="glu",
    )(z, w_glu, w_glu)


def _merge_kernel(a_ref, s_ref, ga_ref, gs_ref, wa_ref, ws_ref, o_ref):
    a = jnp.dot(a_ref[...], wa_ref[...], preferred_element_type=F32)
    s = jnp.dot(s_ref[...], ws_ref[...], preferred_element_type=F32)
    o_ref[...] = (ga_ref[...].astype(F32) * a + gs_ref[...].astype(F32) * s).astype(o_ref.dtype)


def _merge(attn, ssm, gates, w_ab, w_sb, *, tm=1024, tn=1024):
    t = attn.shape[0]
    n = w_ab.shape[1]
    tm, tn = min(tm, t), min(tn, n)
    nb = n // tn
    return pl.pallas_call(
        _merge_kernel, grid=(t // tm, nb),
        in_specs=[pl.BlockSpec((tm, attn.shape[1]), lambda i, j: (i, 0)),
                  pl.BlockSpec((tm, ssm.shape[1]), lambda i, j: (i, 0)),
                  pl.BlockSpec((tm, tn), lambda i, j: (i, j)),
                  pl.BlockSpec((tm, tn), lambda i, j: (i, j + nb)),
                  pl.BlockSpec((w_ab.shape[0], tn), lambda i, j: (0, j)),
                  pl.BlockSpec((w_sb.shape[0], tn), lambda i, j: (0, j))],
        out_specs=pl.BlockSpec((tm, tn), lambda i, j: (i, j)),
        out_shape=jax.ShapeDtypeStruct((t, n), BF16),
        compiler_params=_params("parallel", "parallel"), name="merge",
    )(attn, ssm, gates, gates, w_ab, w_sb)


def _alibi_slopes():
    return [float(v) for v in
            np.asarray(2.0 ** (-8.0 * np.arange(1, N_Q_HEADS + 1) / N_Q_HEADS), dtype=np.float32)]


def _attn_kernel(sink_ref, q0_ref, q1_ref, q2_ref, q3_ref, kc_ref, kp_ref, vc_ref, vp_ref, o_ref):
    n = pl.program_id(1)
    qi = lax.broadcasted_iota(jnp.int32, (BLOCK, 2 * BLOCK), 0)
    sj = lax.broadcasted_iota(jnp.int32, (BLOCK, 2 * BLOCK), 1)
    dist = qi - sj + BLOCK
    valid = (dist >= 0) & (dist < WINDOW) & ((n > 0) | (sj >= BLOCK))
    distf = dist.astype(F32)
    slopes = _alibi_slopes()
    scale = HEAD_DIM ** -0.5
    for k, q_ref in enumerate((q0_ref, q1_ref, q2_ref, q3_ref)):
        ksl = slice(k * HEAD_DIM, (k + 1) * HEAD_DIM)
        kw = jnp.concatenate([kp_ref[:, ksl], kc_ref[:, ksl]], axis=0)
        vw = jnp.concatenate([vp_ref[:, ksl], vc_ref[:, ksl]], axis=0)
        q4 = jnp.concatenate([q_ref[:, g * HEAD_DIM:(g + 1) * HEAD_DIM]
                              for g in range(Q_PER_KV)], axis=0)
        logits = lax.dot_general(q4, kw, (((1,), (1,)), ((), ())),
                                 preferred_element_type=F32) * scale
        ps, dens = [], []
        for g in range(Q_PER_KV):
            h = k * Q_PER_KV + g
            lg = logits[g * BLOCK:(g + 1) * BLOCK]
            lg = jnp.where(valid, lg - slopes[h] * distf, NEG_INF)
            sink = sink_ref[h]
            m = jnp.maximum(jnp.max(lg, axis=-1, keepdims=True), sink)
            p = jnp.exp(lg - m)
            dens.append(jnp.sum(p, axis=-1, keepdims=True) + jnp.exp(sink - m))
            ps.append(p.astype(BF16))
        pv = jnp.dot(jnp.concatenate(ps, axis=0), vw, preferred_element_type=F32)
        for g in range(Q_PER_KV):
            h = k * Q_PER_KV + g
            o_ref[:, h * HEAD_DIM:(h + 1) * HEAD_DIM] = (
                pv[g * BLOCK:(g + 1) * BLOCK] / dens[g]).astype(o_ref.dtype)


def _attention(qkv, sinks, *, bsz, seq):
    aw = N_Q_HEADS * HEAD_DIM
    kvw = N_KV_HEADS * HEAD_DIM
    assert aw == N_KV_HEADS * kvw
    nb = seq // BLOCK
    koff, voff = aw // kvw, aw // kvw + 1
    q_specs = [pl.BlockSpec((BLOCK, kvw), lambda b, n, g=g: (b * nb + n, g))
               for g in range(N_KV_HEADS)]
    prev = lambda b, n: b * nb + jnp.maximum(n - 1, 0)
    return pl.pallas_call(
        _attn_kernel, grid=(bsz, nb),
        in_specs=[pl.BlockSpec(memory_space=pltpu.SMEM)] + q_specs + [
            pl.BlockSpec((BLOCK, kvw), lambda b, n: (b * nb + n, koff)),
            pl.BlockSpec((BLOCK, kvw), lambda b, n: (prev(b, n), koff)),
            pl.BlockSpec((BLOCK, kvw), lambda b, n: (b * nb + n, voff)),
            pl.BlockSpec((BLOCK, kvw), lambda b, n: (prev(b, n), voff))],
        out_specs=pl.BlockSpec((BLOCK, aw), lambda b, n: (b * nb + n, 0)),
        out_shape=jax.ShapeDtypeStruct((bsz * seq, aw), BF16),
        compiler_params=_params("parallel", "parallel"), name="swa_attention",
    )(sinks.astype(F32), *([qkv] * (N_KV_HEADS + 4)))


S5_SLICE_GROUPS = 16
S5_SLICE_IN = S5_SLICE_GROUPS * SSM_GROUP
S5_SLICE_STATE = S5_SLICE_GROUPS * SSM_STATE


S5_C_ROWS = 256


def _s5_kernel(u_ref, bd_ref, lam_ref, cd_ref, d_ref, z_ref, bu0, bu1, ui0, ui1, zi_ref, st_ref,
               *, steps):
    c = pl.program_id(1)
    ns = S5_SLICE_STATE
    n_slab = S5_SLICE_IN // LANES
    rows = steps * SUBLANES

    @pl.when(c == 0)
    def _():
        bu1[...] = jnp.zeros_like(bu1)
        ui1[...] = jnp.zeros_like(ui1)

    @pl.when(c <= 1)
    def _():
        st_ref[...] = jnp.zeros_like(st_ref)

    def stage(bu_cur, bu_nxt, ui_cur, ui_nxt):
        for b in range(SUBLANES):
            for sl in range(n_slab):
                ui_nxt[sl, pl.ds(b, steps, stride=SUBLANES), :] = (
                    u_ref[b, :, sl * LANES:(sl + 1) * LANES])
        u_n = jnp.concatenate([ui_nxt[sl] for sl in range(n_slab)], axis=-1)
        bu_nxt[...] = jnp.dot(u_n.astype(BF16), bd_ref[0], preferred_element_type=F32)

        lr = jnp.broadcast_to(lam_ref[0, 0:1, :], (SUBLANES, ns))
        li = jnp.broadcast_to(lam_ref[0, 1:2, :], (SUBLANES, ns))
        xr, xi = st_ref[:, 0:ns], st_ref[:, ns:2 * ns]
        for t in range(steps):
            rs = slice(t * SUBLANES, (t + 1) * SUBLANES)
            nr = lr * xr - li * xi + bu_cur[rs, 0:ns]
            ni = lr * xi + li * xr + bu_cur[rs, ns:2 * ns]
            bu_cur[rs, 0:ns] = nr
            bu_cur[rs, ns:2 * ns] = ni
            xr, xi = nr, ni
        st_ref[:, 0:ns] = xr
        st_ref[:, ns:2 * ns] = xi

        for k in range(rows // S5_C_ROWS):
            rs = slice(k * S5_C_ROWS, (k + 1) * S5_C_ROWS)
            y = jnp.dot(bu_cur[rs, :].astype(BF16), cd_ref[0], preferred_element_type=F32)
            u_c = jnp.concatenate([ui_cur[sl, rs, :] for sl in range(n_slab)], axis=-1)
            z = jax.nn.gelu(y + d_ref[0] * u_c)
            for sl in range(n_slab):
                zi_ref[sl, rs, :] = z[:, sl * LANES:(sl + 1) * LANES]
        for b in range(SUBLANES):
            for sl in range(n_slab):
                z_ref[b, :, sl * LANES:(sl + 1) * LANES] = (
                    zi_ref[sl, pl.ds(b, steps, stride=SUBLANES), :].astype(z_ref.dtype))

    @pl.when(lax.rem(c, 2) == 0)
    def _():
        stage(bu1, bu0, ui1, ui0)

    @pl.when(lax.rem(c, 2) == 1)
    def _():
        stage(bu0, bu1, ui0, ui1)


def _s5_prepare(a_re, a_im, log_step, b_re, b_im, c_re, c_im):
    g = a_re.shape[0]
    ns = g // S5_SLICE_GROUPS
    ar, ai = a_re.astype(F32), a_im.astype(F32)
    dt = jnp.exp(log_step.astype(F32))[:, None]
    mag = jnp.exp(ar * dt)
    lr = mag * jnp.cos(ai * dt)
    li = mag * jnp.sin(ai * dt)
    den = ar * ar + ai * ai
    nr = lr - 1.0
    zr = (nr * ar + li * ai) / den
    zi = (li * ar - nr * ai) / den
    br, bi = b_re.astype(F32), b_im.astype(F32)
    bbar_r = zr[..., None] * br - zi[..., None] * bi
    bbar_i = zr[..., None] * bi + zi[..., None] * br
    eye = jnp.eye(S5_SLICE_GROUPS, dtype=F32)

    def bdiag(m):
        m = m.reshape(ns, S5_SLICE_GROUPS, SSM_STATE, SSM_GROUP)
        return jnp.einsum("ngph,gk->nghkp", m, eye).reshape(ns, S5_SLICE_IN, S5_SLICE_STATE)

    def cdiag(m):
        m = m.reshape(ns, S5_SLICE_GROUPS, SSM_GROUP, SSM_STATE)
        return jnp.einsum("nghp,gk->nkpgh", m, eye).reshape(ns, S5_SLICE_STATE, S5_SLICE_IN)

    bd = jnp.concatenate([bdiag(bbar_r), bdiag(bbar_i)], axis=-1).astype(BF16)
    cd = jnp.concatenate([cdiag(c_re.astype(F32)), -cdiag(c_im.astype(F32))], axis=1).astype(BF16)
    lam = jnp.stack([lr.reshape(ns, S5_SLICE_STATE), li.reshape(ns, S5_SLICE_STATE)], axis=1)
    return bd, lam, cd


def _s5(u, prep, d_skip, *, bsz, seq, steps=128):
    assert u.shape[:2] == (bsz, seq) and bsz == SUBLANES
    bd, lam, cd = prep
    ns = bd.shape[0]
    steps = min(steps, seq)
    rows = steps * bsz
    width = u.shape[2]
    n_chunks = seq // steps
    assert rows % S5_C_ROWS == 0
    in_blk = pl.BlockSpec((bsz, steps, S5_SLICE_IN),
                          lambda j, c: (0, jnp.minimum(c, n_chunks - 1), j))
    out_blk = pl.BlockSpec((bsz, steps, S5_SLICE_IN), lambda j, c: (0, jnp.maximum(c - 1, 0), j))
    slab = pltpu.VMEM((S5_SLICE_IN // LANES, rows, LANES), F32)
    wide = pltpu.VMEM((rows, 2 * S5_SLICE_STATE), F32)
    return pl.pallas_call(
        functools.partial(_s5_kernel, steps=steps),
        grid=(ns, n_chunks + 1),
        in_specs=[in_blk,
                  pl.BlockSpec((1, S5_SLICE_IN, 2 * S5_SLICE_STATE), lambda j, c: (j, 0, 0)),
                  pl.BlockSpec((1, 2, S5_SLICE_STATE), lambda j, c: (j, 0, 0)),
                  pl.BlockSpec((1, 2 * S5_SLICE_STATE, S5_SLICE_IN), lambda j, c: (j, 0, 0)),
                  pl.BlockSpec((1, 1, S5_SLICE_IN), lambda j, c: (j, 0, 0))],
        out_specs=out_blk,
        out_shape=jax.ShapeDtypeStruct((bsz, seq, width), BF16),
        scratch_shapes=[wide, wide, slab, slab, slab,
                        pltpu.VMEM((SUBLANES, 2 * S5_SLICE_STATE), F32)],
        compiler_params=_params("parallel", "arbitrary"), name="s5_scan",
    )(u, bd, lam, cd, d_skip.astype(F32).reshape(ns, 1, S5_SLICE_IN))


ROUTE_KEEP = PEER_TOPK + 1
ROUTE_ROWS = 24
PEER_BLOCK = 512


def _sorting_network(n):
    pairs = []
    p = 1
    while p < n:
        k = p
        while k >= 1:
            for j in range(k % p, n - k, 2 * k):
                for i in range(min(k, n - j - k)):
                    if (i + j) // (2 * p) == (i + j + k) // (2 * p):
                        pairs.append((i + j, i + j + k))
            k //= 2
        p *= 2
    return pairs


def _extract_sorted(s, count):
    n = s.shape[0] // SUBLANES
    v = [s[r * SUBLANES:(r + 1) * SUBLANES] for r in range(n)]
    wires = 1 << (n - 1).bit_length()
    v += [None] * (wires - n)
    for i, j in _sorting_network(wires):
        if v[j] is None:
            continue
        if v[i] is None:
            v[i], v[j] = v[j], None
        else:
            v[i], v[j] = jnp.maximum(v[i], v[j]), jnp.minimum(v[i], v[j])
    v = v[:n]
    sub = lax.broadcasted_iota(jnp.int32, v[0].shape, 0).astype(F32)
    out = []
    for k in range(count):
        m = jnp.max(v[0], axis=0, keepdims=True)
        out.append(m)
        first = jnp.min(jnp.where(v[0] == m, sub, float(SUBLANES)), axis=0, keepdims=True)
        pop = sub == first
        for r in range(min(n, count - k - 1)):
            nxt = v[r + 1] if r + 1 < n else -jnp.inf
            v[r] = jnp.where(pop, nxt, v[r])
    return out


def _route_kernel(q_ref, keys_ref, theta_ref, coef_ref, s2_ref, e2_ref, a_scr, b_scr, *, tt,
                  rows_per_block):
    for h in range(PEER_HEADS):
        subs = []
        for c in range(2):
            qh = q_ref[:, (2 * h + c) * N_KEYS:(2 * h + c + 1) * N_KEYS].astype(BF16)
            subs.append(lax.dot_general(keys_ref[h, c], qh, (((1,), (1,)), ((), ())),
                                        preferred_element_type=F32))
        s1, s2 = subs
        a_scr[...] = jnp.full(a_scr.shape, -jnp.inf, F32)
        b_scr[...] = jnp.full(b_scr.shape, -jnp.inf, F32)
        a_rows = _extract_sorted(s1, ROUTE_KEEP)
        b_rows = _extract_sorted(s2, ROUTE_KEEP)
        for r in range(ROUTE_KEEP):
            a_scr[r:r + 1, :] = a_rows[r]
            b_scr[r:r + 1, :] = b_rows[r]
        cand = [a_rows[0] + b_scr[...]]
        for k in range(1, SUBLANES):
            cand.append(a_rows[k] + b_scr[0:SUBLANES, :])
        cand.append(a_scr[SUBLANES:ROUTE_ROWS, :] + b_rows[0])
        top = _extract_sorted(jnp.concatenate(cand, axis=0), ROUTE_KEEP)
        zsum = jnp.zeros((1, tt), F32)
        for r in range(PEER_TOPK):
            zsum = zsum + jnp.exp(top[r] - top[0])
        thr = 0.5 * (top[PEER_TOPK - 1] + top[PEER_TOPK])
        theta = thr - s1
        coef = jnp.exp(s1 - a_rows[0]) / zsum
        for g in range(N_KEYS // rows_per_block):
            rs = slice(g * rows_per_block, (g + 1) * rows_per_block)
            theta_ref[h, g] = theta[rs]
            coef_ref[h, g] = coef[rs]
        s2_ref[h] = s2
        e2_ref[h] = jnp.exp(s2 - b_rows[0])


def _route(qp, keys, *, rows_per_block, tt=256):
    t = qp.shape[0]
    tt = min(tt, t)
    n_grp = N_KEYS // rows_per_block
    shp = jax.ShapeDtypeStruct((PEER_HEADS, N_KEYS, t), F32)
    ospec = pl.BlockSpec((PEER_HEADS, N_KEYS, tt), lambda i: (0, 0, i))
    shp_i = jax.ShapeDtypeStruct((PEER_HEADS, n_grp, rows_per_block, t), F32)
    ospec_i = pl.BlockSpec((PEER_HEADS, n_grp, rows_per_block, tt), lambda i: (0, 0, 0, i))
    return pl.pallas_call(
        functools.partial(_route_kernel, tt=tt, rows_per_block=rows_per_block),
        grid=(t // tt,),
        in_specs=[pl.BlockSpec((tt, qp.shape[1]), lambda i: (i, 0)),
                  pl.BlockSpec(keys.shape, lambda i: (0, 0, 0, 0))],
        out_specs=[ospec_i, ospec_i, ospec, ospec], out_shape=[shp_i, shp_i, shp, shp],
        scratch_shapes=[pltpu.VMEM((ROUTE_ROWS, tt), F32), pltpu.VMEM((ROUTE_ROWS, tt), F32)],
        compiler_params=_params("parallel"), name="peer_route",
    )(qp, keys)


def _peer_kernel(xt_ref, u_ref, vt_ref, theta_ref, coef_ref, s2_ref, e2_ref, o_ref,
                 ht_scr, ht_next, hg_scr, *, eb, tm, n_blk):
    s = pl.program_id(0)

    @pl.when(s == 0)
    def _():
        ht_scr[...] = jnp.zeros_like(ht_scr)

    @pl.when(lax.rem(jnp.maximum(s - 1, 0), n_blk) == 0)
    def _():
        o_ref[...] = jnp.zeros_like(o_ref)

    n_half = 2
    for c in range(n_half):
        ts = slice(c * (tm // n_half), (c + 1) * (tm // n_half))
        ht_next[:, ts] = jnp.dot(u_ref[...], xt_ref[:, ts], preferred_element_type=F32)
        es = slice(c * (eb // n_half), (c + 1) * (eb // n_half))
        for ii in range(es.start // N_KEYS, es.stop // N_KEYS):
            rs = slice(ii * N_KEYS, (ii + 1) * N_KEYS)
            for lc in range(tm // LANES):
                ls = slice(lc * LANES, (lc + 1) * LANES)
                g = jnp.zeros((N_KEYS, LANES), F32)
                for h in range(PEER_HEADS):
                    th = theta_ref[h, 0, ii:ii + 1, ls]
                    cf = coef_ref[h, 0, ii:ii + 1, ls]
                    g = g + jnp.where(s2_ref[h, :, ls] >= th, e2_ref[h, :, ls] * cf, 0.0)
                hg_scr[rs, ls] = (jax.nn.gelu(ht_scr[rs, ls]) * g).astype(BF16)
        o_ref[...] += jnp.dot(vt_ref[:, es], hg_scr[es, :], preferred_element_type=F32)
    ht_scr[...] = ht_next[...]


def _peer(xnt, u_tab, vt_tab, route, *, tm=512, eb=PEER_BLOCK):
    d, t = xnt.shape
    n_blk = u_tab.shape[0] // eb
    tm = min(tm, t)
    n_i = eb // N_KEYS
    theta, coef, s2, e2 = route
    assert theta.shape[1:3] == (N_KEYS // n_i, n_i)
    n_pairs = (t // tm) * n_blk
    cur = lambda s: jnp.minimum(s, n_pairs - 1)
    prv = lambda s: jnp.maximum(s - 1, 0)
    tile = lambda p: p // n_blk
    blk = lambda p: p % n_blk
    ispec = pl.BlockSpec((PEER_HEADS, 1, n_i, tm), lambda s: (0, blk(prv(s)), 0, tile(prv(s))))
    rspec = pl.BlockSpec((PEER_HEADS, N_KEYS, tm), lambda s: (0, 0, tile(prv(s))))
    return pl.pallas_call(
        functools.partial(_peer_kernel, eb=eb, tm=tm, n_blk=n_blk), grid=(n_pairs + 1,),
        in_specs=[pl.BlockSpec((d, tm), lambda s: (0, tile(cur(s)))),
                  pl.BlockSpec((eb, d), lambda s: (blk(cur(s)), 0)),
                  pl.BlockSpec((d, eb), lambda s: (0, blk(prv(s)))),
                  ispec, ispec, rspec, rspec],
        out_specs=pl.BlockSpec((d, tm), lambda s: (0, tile(prv(s)))),
        out_shape=jax.ShapeDtypeStruct((d, t), F32),
        scratch_shapes=[pltpu.VMEM((eb, tm), F32), pltpu.VMEM((eb, tm), F32),
                        pltpu.VMEM((eb, tm), BF16)],
        compiler_params=_params("arbitrary"), name="peer_experts",
    )(xnt, u_tab, vt_tab, theta, coef, s2, e2)


def kernel(x, ln_mix, w_in, ssm_a_re, ssm_a_im, ssm_log_step, ssm_b_re, ssm_b_im, ssm_c_re,
           ssm_c_im, ssm_d, w_glu, attn_sinks, w_attn_branch, w_ssm_branch, w_out, ln_ffn,
           w_peer_q, peer_keys, peer_u, peer_v, ln_final):
    bsz, seq, d = x.shape
    depth = w_in.shape[0]
    aw = N_Q_HEADS * HEAD_DIM
    qkv_w = aw + 2 * N_KV_HEADS * HEAD_DIM
    ssm_w = ssm_d.shape[-1]

    h = x.reshape(bsz * seq, d)
    pt = None
    for l in range(depth):
        if l == 0:
            xres = h
            (xn,) = _addnorm(h, ln_mix[l])
        else:
            xres, xn = _addnorm(h, ln_mix[l], delta_t=pt, emit_sum=True)
        w_l = _cast_bf16(w_in, l)
        qkv = _matmul(xn, w_l, n=qkv_w, out_dtype=BF16)
        u = _matmul(xn, w_l, col0=qkv_w, n=ssm_w, out_dtype=F32, seq=seq)
        gates = _matmul(xn, w_l, col0=qkv_w + ssm_w, n=2 * d, out_dtype=BF16, epilogue="sigmoid")
        attn = _attention(qkv, attn_sinks[l], bsz=bsz, seq=seq)
        prep = _s5_prepare(ssm_a_re[l], ssm_a_im[l], ssm_log_step[l], ssm_b_re[l], ssm_b_im[l],
                           ssm_c_re[l], ssm_c_im[l])
        z = _s5(u, prep, ssm_d[l], bsz=bsz, seq=seq)
        ssm = _glu(z, _cast_bf16(w_glu, l))
        merged = _merge(attn, ssm, gates, _cast_bf16(w_attn_branch, l),
                        _cast_bf16(w_ssm_branch, l))
        h = _matmul(merged, _cast_bf16(w_out, l), out_dtype=F32, epilogue="residual",
                    residual=xres)
        hn, hnt = _addnorm(h, ln_ffn[l], emit_norm_t=True)
        qp = _matmul(hn, _cast_bf16(w_peer_q, l), out_dtype=F32)
        route = _route(qp, peer_keys[l].astype(BF16), rows_per_block=PEER_BLOCK // N_KEYS)
        pt = _peer(hnt, _cast_bf16(peer_u, l), _cast_bf16(peer_v, l, transpose=True), route)
    (out,) = _addnorm(h, ln_final, delta_t=pt, emit_norm_f32=True)
    return out.reshape(bsz, seq, d)
```

```python
import functools

import numpy as np
import jax
import jax.numpy as jnp
from jax import lax
from jax.experimental import pallas as pl
from jax.experimental.pallas import tpu as pltpu

F32 = jnp.float32
BF16 = jnp.bfloat16

HEAD_DIM = 128
N_Q_HEADS = 16
N_KV_HEADS = 4
Q_PER_KV = 4
WINDOW = 128
BLOCK = 128
NEG_INF = -1e30
SSM_GROUP = 16
SSM_STATE = 64
PEER_HEADS = 8
N_KEYS = 128
PEER_TOPK = 16
NORM_EPS = 1e-6

V7X_VMEM_BYTES = 64 * 1024 * 1024
VMEM_LIMIT = V7X_VMEM_BYTES - 4 * 1024 * 1024
V7X_MXU_DIM = 256
SUBLANES = 8
LANES = 128


def _params(*sem, flags=None):
    return pltpu.CompilerParams(dimension_semantics=sem, vmem_limit_bytes=VMEM_LIMIT, flags=flags)


CAST_BLOCK_ELEMS = 2 * 1024 * 1024


def _cast_kernel(w_ref, o_ref, *, transpose):
    w = w_ref[...]
    o_ref[...] = (w.T if transpose else w).astype(o_ref.dtype)


def _cast_bf16(w, l, *, transpose=False):
    _, r, c = w.shape
    rows = r
    while rows * c > CAST_BLOCK_ELEMS and rows % 2 == 0 and rows > LANES:
        rows //= 2
    out_shape, out_spec = ((c, r), pl.BlockSpec((c, rows), lambda i: (0, i))) if transpose else (
        (r, c), pl.BlockSpec((rows, c), lambda i: (i, 0)))
    return pl.pallas_call(
        functools.partial(_cast_kernel, transpose=transpose), grid=(r // rows,),
        in_specs=[pl.BlockSpec((None, rows, c), lambda i: (l, i, 0))],
        out_specs=out_spec, out_shape=jax.ShapeDtypeStruct(out_shape, BF16),
        compiler_params=_params("parallel"), name="cast_bf16",
    )(w)


def _addnorm_kernel(*refs, has_delta, emit_sum, emit_norm_f32, emit_norm_t):
    x = refs[0][...]
    k = 1
    if has_delta:
        x = x + refs[k][...].T
        k += 1
    g_ref = refs[k]
    outs = refs[k + 1:]
    y = x * lax.rsqrt(jnp.mean(x * x, axis=-1, keepdims=True) + NORM_EPS)
    y = y * g_ref[...]
    k = 0
    if emit_sum:
        outs[k][...] = x
        k += 1
    if emit_norm_f32:
        outs[k][...] = y
    else:
        outs[k][...] = y.astype(BF16)
    if emit_norm_t:
        outs[k + 1][...] = y.T.astype(BF16)


def _addnorm(x, gain, *, delta_t=None, emit_sum=False, emit_norm_f32=False, emit_norm_t=False,
             rows=256):
    t, d = x.shape
    rows = min(rows, t)
    spec = pl.BlockSpec((rows, d), lambda i: (i, 0))
    spec_t = pl.BlockSpec((d, rows), lambda i: (0, i))
    args, in_specs = [x], [spec]
    if delta_t is not None:
        args.append(delta_t)
        in_specs.append(spec_t)
    args.append(gain.reshape(1, d).astype(F32))
    in_specs.append(pl.BlockSpec((1, d), lambda i: (0, 0)))
    out_shapes, out_specs = [], []
    if emit_sum:
        out_shapes.append(jax.ShapeDtypeStruct((t, d), F32))
        out_specs.append(spec)
    out_shapes.append(jax.ShapeDtypeStruct((t, d), F32 if emit_norm_f32 else BF16))
    out_specs.append(spec)
    if emit_norm_t:
        out_shapes.append(jax.ShapeDtypeStruct((d, t), BF16))
        out_specs.append(spec_t)
    return pl.pallas_call(
        functools.partial(_addnorm_kernel, has_delta=delta_t is not None, emit_sum=emit_sum,
                          emit_norm_f32=emit_norm_f32, emit_norm_t=emit_norm_t),
        grid=(t // rows,), in_specs=in_specs, out_specs=out_specs, out_shape=out_shapes,
        compiler_params=_params("parallel"), name="addnorm",
    )(*args)


def _mm_kernel(x_ref, w_ref, *rest, epilogue, side_transpose):
    n_side = len(side_transpose)
    ins, outs = rest[:len(rest) - 1 - n_side], rest[len(rest) - 1 - n_side:]
    o_ref = outs[0]
    acc = jnp.dot(x_ref[...], w_ref[...], preferred_element_type=F32)
    if epilogue == "sigmoid":
        acc = jax.nn.sigmoid(acc)
    elif epilogue == "residual":
        acc = acc + ins[0][...]
    o_ref[...] = acc.astype(o_ref.dtype)
    for k, transpose in enumerate(side_transpose):
        cw = ins[len(ins) - n_side + k][...]
        outs[1 + k][...] = (cw.T if transpose else cw).astype(outs[1 + k].dtype)


def _matmul(x, w, *, out_dtype, col0=0, n=None, epilogue=None, residual=None, seq=None,
            side_casts=(), tm=1024, tn=1024):
    t, k = x.shape
    n = w.shape[1] if n is None else n
    tm, tn = min(tm, t), min(tn, n)
    assert col0 % tn == 0 and n % tn == 0
    cb = col0 // tn
    nj = n // tn
    if seq is None:
        out_shape, out_spec = (t, n), pl.BlockSpec((tm, tn), lambda i, j: (i, j))
    else:
        tm = min(tm, seq)
        per = seq // tm
        out_shape = (t // seq, seq, n)
        out_spec = pl.BlockSpec((None, tm, tn), lambda i, j: (i // per, i % per, j))
    n_steps = (t // tm) * nj
    in_specs = [pl.BlockSpec((tm, k), lambda i, j: (i, 0)),
                pl.BlockSpec((k, tn), lambda i, j: (0, j + cb))]
    args = [x, w]
    if epilogue == "residual":
        in_specs.append(pl.BlockSpec((tm, tn), lambda i, j: (i, j)))
        args.append(residual)
    out_shapes = [jax.ShapeDtypeStruct(out_shape, out_dtype)]
    out_specs = [out_spec]
    for cw, layer, transpose in side_casts:
        _, r, c = cw.shape
        rows = r // n_steps
        assert r % n_steps == 0 and rows % LANES == 0
        in_specs.append(pl.BlockSpec((None, rows, c), lambda i, j, layer=layer: (layer, i * nj + j, 0)))
        args.append(cw)
        if transpose:
            out_shapes.append(jax.ShapeDtypeStruct((c, r), BF16))
            out_specs.append(pl.BlockSpec((c, rows), lambda i, j: (0, i * nj + j)))
        else:
            out_shapes.append(jax.ShapeDtypeStruct((r, c), BF16))
            out_specs.append(pl.BlockSpec((rows, c), lambda i, j: (i * nj + j, 0)))
    outs = pl.pallas_call(
        functools.partial(_mm_kernel, epilogue=epilogue,
                          side_transpose=tuple(tr for _, _, tr in side_casts)),
        grid=(t // tm, nj), in_specs=in_specs, out_specs=out_specs, out_shape=out_shapes,
        compiler_params=_params("parallel", "parallel"), name="matmul",
    )(*args)
    return outs if side_casts else outs[0]


def _glu_kernel(z_ref, wa_ref, wb_ref, o_ref):
    z = z_ref[...]
    a = jnp.dot(z, wa_ref[...], preferred_element_type=F32)
    b = jnp.dot(z, wb_ref[...], preferred_element_type=F32)
    o_ref[...] = (a * jax.nn.sigmoid(b)).astype(o_ref.dtype)


def _glu(z, w_glu, *, tm=1024, tn=512):
    bsz, seq, k = z.shape
    t = bsz * seq
    n = w_glu.shape[1] // 2
    tm, tn = min(tm, seq), min(tn, n)
    per = seq // tm
    nb = n // tn
    return pl.pallas_call(
        _glu_kernel, grid=(t // tm, nb),
        in_specs=[pl.BlockSpec((None, tm, k), lambda i, j: (i // per, i % per, 0)),
                  pl.BlockSpec((k, tn), lambda i, j: (0, j)),
                  pl.BlockSpec((k, tn), lambda i, j: (0, j + nb))],
        out_specs=pl.BlockSpec((tm, tn), lambda i, j: (i, j)),
        out_shape=jax.ShapeDtypeStruct((t, n), BF16),
        compiler_params=_params("parallel", "parallel"), name="glu",
    )(z, w_glu, w_glu)


def _merge_kernel(a_ref, s_ref, ga_ref, gs_ref, wa_ref, ws_ref, o_ref):
    a = jnp.dot(a_ref[...], wa_ref[...], preferred_element_type=F32)
    s = jnp.dot(s_ref[...], ws_ref[...], preferred_element_type=F32)
    o_ref[...] = (ga_ref[...].astype(F32) * a + gs_ref[...].astype(F32) * s).astype(o_ref.dtype)


def _merge(attn, ssm, gates, w_ab, w_sb, *, tm=1024, tn=1024):
    t = attn.shape[0]
    n = w_ab.shape[1]
    tm, tn = min(tm, t), min(tn, n)
    nb = n // tn
    return pl.pallas_call(
        _merge_kernel, grid=(t // tm, nb),
        in_specs=[pl.BlockSpec((tm, attn.shape[1]), lambda i, j: (i, 0)),
                  pl.BlockSpec((tm, ssm.shape[1]), lambda i, j: (i, 0)),
                  pl.BlockSpec((tm, tn), lambda i, j: (i, j)),
                  pl.BlockSpec((tm, tn), lambda i, j: (i, j + nb)),
                  pl.BlockSpec((w_ab.shape[0], tn), lambda i, j: (0, j)),
                  pl.BlockSpec((w_sb.shape[0], tn), lambda i, j: (0, j))],
        out_specs=pl.BlockSpec((tm, tn), lambda i, j: (i, j)),
        out_shape=jax.ShapeDtypeStruct((t, n), BF16),
        compiler_params=_params("parallel", "parallel"), name="merge",
    )(attn, ssm, gates, gates, w_ab, w_sb)


def _alibi_slopes():
    return [float(v) for v in
            np.asarray(2.0 ** (-8.0 * np.arange(1, N_Q_HEADS + 1) / N_Q_HEADS), dtype=np.float32)]


def _attn_kernel(sink_ref, q0_ref, q1_ref, q2_ref, q3_ref, kc_ref, kp_ref, vc_ref, vp_ref, o_ref):
    n = pl.program_id(1)
    qi = lax.broadcasted_iota(jnp.int32, (BLOCK, 2 * BLOCK), 0)
    sj = lax.broadcasted_iota(jnp.int32, (BLOCK, 2 * BLOCK), 1)
    dist = qi - sj + BLOCK
    valid = (dist >= 0) & (dist < WINDOW) & ((n > 0) | (sj >= BLOCK))
    distf = dist.astype(F32)
    slopes = _alibi_slopes()
    scale = HEAD_DIM ** -0.5
    for k, q_ref in enumerate((q0_ref, q1_ref, q2_ref, q3_ref)):
        ksl = slice(k * HEAD_DIM, (k + 1) * HEAD_DIM)
        kw = jnp.concatenate([kp_ref[:, ksl], kc_ref[:, ksl]], axis=0)
        vw = jnp.concatenate([vp_ref[:, ksl], vc_ref[:, ksl]], axis=0)
        q4 = jnp.concatenate([q_ref[:, g * HEAD_DIM:(g + 1) * HEAD_DIM]
                              for g in range(Q_PER_KV)], axis=0)
        logits = lax.dot_general(q4, kw, (((1,), (1,)), ((), ())),
                                 preferred_element_type=F32) * scale
        ps, dens = [], []
        for g in range(Q_PER_KV):
            h = k * Q_PER_KV + g
            lg = logits[g * BLOCK:(g + 1) * BLOCK]
            lg = jnp.where(valid, lg - slopes[h] * distf, NEG_INF)
            sink = sink_ref[h]
            m = jnp.maximum(jnp.max(lg, axis=-1, keepdims=True), sink)
            p = jnp.exp(lg - m)
            dens.append(jnp.sum(p, axis=-1, keepdims=True) + jnp.exp(sink - m))
            ps.append(p.astype(BF16))
        pv = jnp.dot(jnp.concatenate(ps, axis=0), vw, preferred_element_type=F32)
        for g in range(Q_PER_KV):
            h = k * Q_PER_KV + g
            o_ref[:, h * HEAD_DIM:(h + 1) * HEAD_DIM] = (
                pv[g * BLOCK:(g + 1) * BLOCK] / dens[g]).astype(o_ref.dtype)


def _attention(qkv, sinks, *, bsz, seq):
    aw = N_Q_HEADS * HEAD_DIM
    kvw = N_KV_HEADS * HEAD_DIM
    assert aw == N_KV_HEADS * kvw
    nb = seq // BLOCK
    koff, voff = aw // kvw, aw // kvw + 1
    q_specs = [pl.BlockSpec((BLOCK, kvw), lambda b, n, g=g: (b * nb + n, g))
               for g in range(N_KV_HEADS)]
    prev = lambda b, n: b * nb + jnp.maximum(n - 1, 0)
    return pl.pallas_call(
        _attn_kernel, grid=(bsz, nb),
        in_specs=[pl.BlockSpec(memory_space=pltpu.SMEM)] + q_specs + [
            pl.BlockSpec((BLOCK, kvw), lambda b, n: (b * nb + n, koff)),
            pl.BlockSpec((BLOCK, kvw), lambda b, n: (prev(b, n), koff)),
            pl.BlockSpec((BLOCK, kvw), lambda b, n: (b * nb + n, voff)),
            pl.BlockSpec((BLOCK, kvw), lambda b, n: (prev(b, n), voff))],
        out_specs=pl.BlockSpec((BLOCK, aw), lambda b, n: (b * nb + n, 0)),
        out_shape=jax.ShapeDtypeStruct((bsz * seq, aw), BF16),
        compiler_params=_params("parallel", "parallel"), name="swa_attention",
    )(sinks.astype(F32), *([qkv] * (N_KV_HEADS + 4)))


S5_SLICE_GROUPS = 16
S5_SLICE_IN = S5_SLICE_GROUPS * SSM_GROUP
S5_SLICE_STATE = S5_SLICE_GROUPS * SSM_STATE


S5_C_ROWS = 256


def _s5_kernel(u_ref, bd_ref, lam_ref, cd_ref, d_ref, z_ref, bu0, bu1, ui0, ui1, zi_ref, st_ref,
               *, steps):
    c = pl.program_id(1)
    ns = S5_SLICE_STATE
    n_slab = S5_SLICE_IN // LANES
    rows = steps * SUBLANES

    @pl.when(c == 0)
    def _():
        bu1[...] = jnp.zeros_like(bu1)
        ui1[...] = jnp.zeros_like(ui1)

    @pl.when(c <= 1)
    def _():
        st_ref[...] = jnp.zeros_like(st_ref)

    def stage(bu_cur, bu_nxt, ui_cur, ui_nxt):
        for b in range(SUBLANES):
            for sl in range(n_slab):
                ui_nxt[sl, pl.ds(b, steps, stride=SUBLANES), :] = (
                    u_ref[b, :, sl * LANES:(sl + 1) * LANES])
        u_n = jnp.concatenate([ui_nxt[sl] for sl in range(n_slab)], axis=-1)
        bu_nxt[...] = jnp.dot(u_n.astype(BF16), bd_ref[0], preferred_element_type=F32)

        lr = jnp.broadcast_to(lam_ref[0, 0:1, :], (SUBLANES, ns))
        li = jnp.broadcast_to(lam_ref[0, 1:2, :], (SUBLANES, ns))
        xr, xi = st_ref[:, 0:ns], st_ref[:, ns:2 * ns]
        for t in range(steps):
            rs = slice(t * SUBLANES, (t + 1) * SUBLANES)
            nr = lr * xr - li * xi + bu_cur[rs, 0:ns]
            ni = lr * xi + li * xr + bu_cur[rs, ns:2 * ns]
            bu_cur[rs, 0:ns] = nr
            bu_cur[rs, ns:2 * ns] = ni
            xr, xi = nr, ni
        st_ref[:, 0:ns] = xr
        st_ref[:, ns:2 * ns] = xi

        for k in range(rows // S5_C_ROWS):
            rs = slice(k * S5_C_ROWS, (k + 1) * S5_C_ROWS)
            y = jnp.dot(bu_cur[rs, :].astype(BF16), cd_ref[0], preferred_element_type=F32)
            u_c = jnp.concatenate([ui_cur[sl, rs, :] for sl in range(n_slab)], axis=-1)
            z = jax.nn.gelu(y + d_ref[0] * u_c)
            for sl in range(n_slab):
                zi_ref[sl, rs, :] = z[:, sl * LANES:(sl + 1) * LANES]
        for b in range(SUBLANES):
            for sl in range(n_slab):
                z_ref[b, :, sl * LANES:(sl + 1) * LANES] = (
                    zi_ref[sl, pl.ds(b, steps, stride=SUBLANES), :].astype(z_ref.dtype))

    @pl.when(lax.rem(c, 2) == 0)
    def _():
        stage(bu1, bu0, ui1, ui0)

    @pl.when(lax.rem(c, 2) == 1)
    def _():
        stage(bu0, bu1, ui0, ui1)


def _s5_prepare(a_re, a_im, log_step, b_re, b_im, c_re, c_im):
    g = a_re.shape[0]
    ns = g // S5_SLICE_GROUPS
    ar, ai = a_re.astype(F32), a_im.astype(F32)
    dt = jnp.exp(log_step.astype(F32))[:, None]
    mag = jnp.exp(ar * dt)
    lr = mag * jnp.cos(ai * dt)
    li = mag * jnp.sin(ai * dt)
    den = ar * ar + ai * ai
    nr = lr - 1.0
    zr = (nr * ar + li * ai) / den
    zi = (li * ar - nr * ai) / den
    br, bi = b_re.astype(F32), b_im.astype(F32)
    bbar_r = zr[..., None] * br - zi[..., None] * bi
    bbar_i = zr[..., None] * bi + zi[..., None] * br
    eye = jnp.eye(S5_SLICE_GROUPS, dtype=F32)

    def bdiag(m):
        m = m.reshape(ns, S5_SLICE_GROUPS, SSM_STATE, SSM_GROUP)
        return jnp.einsum("ngph,gk->nghkp", m, eye).reshape(ns, S5_SLICE_IN, S5_SLICE_STATE)

    def cdiag(m):
        m = m.reshape(ns, S5_SLICE_GROUPS, SSM_GROUP, SSM_STATE)
        return jnp.einsum("nghp,gk->nkpgh", m, eye).reshape(ns, S5_SLICE_STATE, S5_SLICE_IN)

    bd = jnp.concatenate([bdiag(bbar_r), bdiag(bbar_i)], axis=-1).astype(BF16)
    cd = jnp.concatenate([cdiag(c_re.astype(F32)), -cdiag(c_im.astype(F32))], axis=1).astype(BF16)
    lam = jnp.stack([lr.reshape(ns, S5_SLICE_STATE), li.reshape(ns, S5_SLICE_STATE)], axis=1)
    return bd, lam, cd


def _s5(u, prep, d_skip, *, bsz, seq, steps=128):
    assert u.shape[:2] == (bsz, seq) and bsz == SUBLANES
    bd, lam, cd = prep
    ns = bd.shape[0]
    steps = min(steps, seq)
    rows = steps * bsz
    width = u.shape[2]
    n_chunks = seq // steps
    assert rows % S5_C_ROWS == 0
    in_blk = pl.BlockSpec((bsz, steps, S5_SLICE_IN),
                          lambda j, c: (0, jnp.minimum(c, n_chunks - 1), j))
    out_blk = pl.BlockSpec((bsz, steps, S5_SLICE_IN), lambda j, c: (0, jnp.maximum(c - 1, 0), j))
    slab = pltpu.VMEM((S5_SLICE_IN // LANES, rows, LANES), F32)
    wide = pltpu.VMEM((rows, 2 * S5_SLICE_STATE), F32)
    return pl.pallas_call(
        functools.partial(_s5_kernel, steps=steps),
        grid=(ns, n_chunks + 1),
        in_specs=[in_blk,
                  pl.BlockSpec((1, S5_SLICE_IN, 2 * S5_SLICE_STATE), lambda j, c: (j, 0, 0)),
                  pl.BlockSpec((1, 2, S5_SLICE_STATE), lambda j, c: (j, 0, 0)),
                  pl.BlockSpec((1, 2 * S5_SLICE_STATE, S5_SLICE_IN), lambda j, c: (j, 0, 0)),
                  pl.BlockSpec((1, 1, S5_SLICE_IN), lambda j, c: (j, 0, 0))],
        out_specs=out_blk,
        out_shape=jax.ShapeDtypeStruct((bsz, seq, width), BF16),
        scratch_shapes=[wide, wide, slab, slab, slab,
                        pltpu.VMEM((SUBLANES, 2 * S5_SLICE_STATE), F32)],
        compiler_params=_params("parallel", "arbitrary"), name="s5_scan",
    )(u, bd, lam, cd, d_skip.astype(F32).reshape(ns, 1, S5_SLICE_IN))


ROUTE_KEEP = PEER_TOPK + 1
ROUTE_ROWS = 24
PEER_BLOCK = 512


def _sorting_network(n):
    pairs = []
    p = 1
    while p < n:
        k = p
        while k >= 1:
            for j in range(k % p, n - k, 2 * k):
                for i in range(min(k, n - j - k)):
                    if (i + j) // (2 * p) == (i + j + k) // (2 * p):
                        pairs.append((i + j, i + j + k))
            k //= 2
        p *= 2
    return pairs


def _extract_sorted(s, count):
    n = s.shape[0] // SUBLANES
    v = [s[r * SUBLANES:(r + 1) * SUBLANES] for r in range(n)]
    wires = 1 << (n - 1).bit_length()
    v += [None] * (wires - n)
    for i, j in _sorting_network(wires):
        if v[j] is None:
            continue
        if v[i] is None:
            v[i], v[j] = v[j], None
        else:
            v[i], v[j] = jnp.maximum(v[i], v[j]), jnp.minimum(v[i], v[j])
    v = v[:n]
    sub = lax.broadcasted_iota(jnp.int32, v[0].shape, 0).astype(F32)
    out = []
    for k in range(count):
        m = jnp.max(v[0], axis=0, keepdims=True)
        out.append(m)
        first = jnp.min(jnp.where(v[0] == m, sub, float(SUBLANES)), axis=0, keepdims=True)
        pop = sub == first
        for r in range(min(n, count - k - 1)):
            nxt = v[r + 1] if r + 1 < n else -jnp.inf
            v[r] = jnp.where(pop, nxt, v[r])
    return out


def _route_kernel(q_ref, keys_ref, theta_ref, coef_ref, s2_ref, e2_ref, a_scr, b_scr, *, tt,
                  rows_per_block):
    for h in range(PEER_HEADS):
        subs = []
        for c in range(2):
            qh = q_ref[:, (2 * h + c) * N_KEYS:(2 * h + c + 1) * N_KEYS].astype(BF16)
            subs.append(lax.dot_general(keys_ref[h, c], qh, (((1,), (1,)), ((), ())),
                                        preferred_element_type=F32))
        s1, s2 = subs
        a_scr[...] = jnp.full(a_scr.shape, -jnp.inf, F32)
        b_scr[...] = jnp.full(b_scr.shape, -jnp.inf, F32)
        a_rows = _extract_sorted(s1, ROUTE_KEEP)
        b_rows = _extract_sorted(s2, ROUTE_KEEP)
        for r in range(ROUTE_KEEP):
            a_scr[r:r + 1, :] = a_rows[r]
            b_scr[r:r + 1, :] = b_rows[r]
        cand = [a_rows[0] + b_scr[...]]
        for k in range(1, SUBLANES):
            cand.append(a_rows[k] + b_scr[0:SUBLANES, :])
        cand.append(a_scr[SUBLANES:ROUTE_ROWS, :] + b_rows[0])
        top = _extract_sorted(jnp.concatenate(cand, axis=0), ROUTE_KEEP)
        zsum = jnp.zeros((1, tt), F32)
        for r in range(PEER_TOPK):
            zsum = zsum + jnp.exp(top[r] - top[0])
        thr = 0.5 * (top[PEER_TOPK - 1] + top[PEER_TOPK])
        theta = thr - s1
        coef = jnp.exp(s1 - a_rows[0]) / zsum
        for g in range(N_KEYS // rows_per_block):
            rs = slice(g * rows_per_block, (g + 1) * rows_per_block)
            theta_ref[h, g] = theta[rs]
            coef_ref[h, g] = coef[rs]
        s2_ref[h] = s2
        e2_ref[h] = jnp.exp(s2 - b_rows[0])


def _route(qp, keys, *, rows_per_block, tt=256):
    t = qp.shape[0]
    tt = min(tt, t)
    n_grp = N_KEYS // rows_per_block
    shp = jax.ShapeDtypeStruct((PEER_HEADS, N_KEYS, t), F32)
    ospec = pl.BlockSpec((PEER_HEADS, N_KEYS, tt), lambda i: (0, 0, i))
    shp_i = jax.ShapeDtypeStruct((PEER_HEADS, n_grp, rows_per_block, t), F32)
    ospec_i = pl.BlockSpec((PEER_HEADS, n_grp, rows_per_block, tt), lambda i: (0, 0, 0, i))
    return pl.pallas_call(
        functools.partial(_route_kernel, tt=tt, rows_per_block=rows_per_block),
        grid=(t // tt,),
        in_specs=[pl.BlockSpec((tt, qp.shape[1]), lambda i: (i, 0)),
                  pl.BlockSpec(keys.shape, lambda i: (0, 0, 0, 0))],
        out_specs=[ospec_i, ospec_i, ospec, ospec], out_shape=[shp_i, shp_i, shp, shp],
        scratch_shapes=[pltpu.VMEM((ROUTE_ROWS, tt), F32), pltpu.VMEM((ROUTE_ROWS, tt), F32)],
        compiler_params=_params("parallel"), name="peer_route",
    )(qp, keys)


def _peer_kernel(xt_ref, u_ref, vt_ref, theta_ref, coef_ref, s2_ref, e2_ref, o_ref,
                 ht_scr, ht_next, hg_scr, *, eb, tm, n_blk):
    s = pl.program_id(0)

    @pl.when(s == 0)
    def _():
        ht_scr[...] = jnp.zeros_like(ht_scr)

    @pl.when(lax.rem(jnp.maximum(s - 1, 0), n_blk) == 0)
    def _():
        o_ref[...] = jnp.zeros_like(o_ref)

    n_half = 2
    for c in range(n_half):
        ts = slice(c * (tm // n_half), (c + 1) * (tm // n_half))
        ht_next[:, ts] = jnp.dot(u_ref[...], xt_ref[:, ts], preferred_element_type=F32)
        es = slice(c * (eb // n_half), (c + 1) * (eb // n_half))
        for ii in range(es.start // N_KEYS, es.stop // N_KEYS):
            rs = slice(ii * N_KEYS, (ii + 1) * N_KEYS)
            for lc in range(tm // LANES):
                ls = slice(lc * LANES, (lc + 1) * LANES)
                g = jnp.zeros((N_KEYS, LANES), F32)
                for h in range(PEER_HEADS):
                    th = theta_ref[h, 0, ii:ii + 1, ls]
                    cf = coef_ref[h, 0, ii:ii + 1, ls]
                    g = g + jnp.where(s2_ref[h, :, ls] >= th, e2_ref[h, :, ls] * cf, 0.0)
                hg_scr[rs, ls] = (jax.nn.gelu(ht_scr[rs, ls]) * g).astype(BF16)
        o_ref[...] += jnp.dot(vt_ref[:, es], hg_scr[es, :], preferred_element_type=F32)
    ht_scr[...] = ht_next[...]


def _peer(xnt, u_tab, vt_tab, route, *, tm=512, eb=PEER_BLOCK):
    d, t = xnt.shape
    n_blk = u_tab.shape[0] // eb
    tm = min(tm, t)
    n_i = eb // N_KEYS
    theta, coef, s2, e2 = route
    assert theta.shape[1:3] == (N_KEYS // n_i, n_i)
    n_pairs = (t // tm) * n_blk
    cur = lambda s: jnp.minimum(s, n_pairs - 1)
    prv = lambda s: jnp.maximum(s - 1, 0)
    tile = lambda p: p // n_blk
    blk = lambda p: p % n_blk
    ispec = pl.BlockSpec((PEER_HEADS, 1, n_i, tm), lambda s: (0, blk(prv(s)), 0, tile(prv(s))))
    rspec = pl.BlockSpec((PEER_HEADS, N_KEYS, tm), lambda s: (0, 0, tile(prv(s))))
    return pl.pallas_call(
        functools.partial(_peer_kernel, eb=eb, tm=tm, n_blk=n_blk), grid=(n_pairs + 1,),
        in_specs=[pl.BlockSpec((d, tm), lambda s: (0, tile(cur(s)))),
                  pl.BlockSpec((eb, d), lambda s: (blk(cur(s)), 0)),
                  pl.BlockSpec((d, eb), lambda s: (0, blk(prv(s)))),
                  ispec, ispec, rspec, rspec],
        out_specs=pl.BlockSpec((d, tm), lambda s: (0, tile(prv(s)))),
        out_shape=jax.ShapeDtypeStruct((d, t), F32),
        scratch_shapes=[pltpu.VMEM((eb, tm), F32), pltpu.VMEM((eb, tm), F32),
                        pltpu.VMEM((eb, tm), BF16)],
        compiler_params=_params("arbitrary"), name="peer_experts",
    )(xnt, u_tab, vt_tab, theta, coef, s2, e2)


def _side_cast_fits(t, n, table, tm=1024, tn=1024):
    n_steps = (t // min(tm, t)) * (n // min(tn, n))
    _, r, c = table.shape
    rows = r // n_steps
    return r % n_steps == 0 and rows % LANES == 0 and rows * c <= CAST_BLOCK_ELEMS // 4


def kernel(x, ln_mix, w_in, ssm_a_re, ssm_a_im, ssm_log_step, ssm_b_re, ssm_b_im, ssm_c_re,
           ssm_c_im, ssm_d, w_glu, attn_sinks, w_attn_branch, w_ssm_branch, w_out, ln_ffn,
           w_peer_q, peer_keys, peer_u, peer_v, ln_final):
    bsz, seq, d = x.shape
    depth = w_in.shape[0]
    aw = N_Q_HEADS * HEAD_DIM
    qkv_w = aw + 2 * N_KV_HEADS * HEAD_DIM
    ssm_w = ssm_d.shape[-1]

    h = x.reshape(bsz * seq, d)
    pt = None
    for l in range(depth):
        if l == 0:
            xres = h
            (xn,) = _addnorm(h, ln_mix[l])
        else:
            xres, xn = _addnorm(h, ln_mix[l], delta_t=pt, emit_sum=True)
        w_l = _cast_bf16(w_in, l)
        qkv = _matmul(xn, w_l, n=qkv_w, out_dtype=BF16)
        u = _matmul(xn, w_l, col0=qkv_w, n=ssm_w, out_dtype=F32, seq=seq)
        side = [(peer_u, l, False), (peer_v, l, True)] if _side_cast_fits(bsz * seq, 2 * d, peer_u) else []
        gates = _matmul(xn, w_l, col0=qkv_w + ssm_w, n=2 * d, out_dtype=BF16, epilogue="sigmoid",
                        side_casts=side)
        if side:
            gates, u_tab, vt_tab = gates
        else:
            u_tab, vt_tab = _cast_bf16(peer_u, l), _cast_bf16(peer_v, l, transpose=True)
        attn = _attention(qkv, attn_sinks[l], bsz=bsz, seq=seq)
        prep = _s5_prepare(ssm_a_re[l], ssm_a_im[l], ssm_log_step[l], ssm_b_re[l], ssm_b_im[l],
                           ssm_c_re[l], ssm_c_im[l])
        z = _s5(u, prep, ssm_d[l], bsz=bsz, seq=seq)
        ssm = _glu(z, _cast_bf16(w_glu, l))
        merged = _merge(attn, ssm, gates, _cast_bf16(w_attn_branch, l),
                        _cast_bf16(w_ssm_branch, l))
        h = _matmul(merged, _cast_bf16(w_out, l), out_dtype=F32, epilogue="residual",
                    residual=xres)
        hn, hnt = _addnorm(h, ln_ffn[l], emit_norm_t=True)
        qp = _matmul(hn, _cast_bf16(w_peer_q, l), out_dtype=F32)
        route = _route(qp, peer_keys[l].astype(BF16), rows_per_block=PEER_BLOCK // N_KEYS)
        pt = _peer(hnt, u_tab, vt_tab, route)
    (out,) = _addnorm(h, ln_final, delta_t=pt, emit_norm_f32=True)
    return out.reshape(bsz, seq, d)
```

```python
import functools

import numpy as np
import jax
import jax.numpy as jnp
from jax import lax
from jax.experimental import pallas as pl
from jax.experimental.pallas import tpu as pltpu

F32 = jnp.float32
BF16 = jnp.bfloat16

HEAD_DIM = 128
N_Q_HEADS = 16
N_KV_HEADS = 4
Q_PER_KV = 4
WINDOW = 128
BLOCK = 128
NEG_INF = -1e30
SSM_GROUP = 16
SSM_STATE = 64
PEER_HEADS = 8
N_KEYS = 128
PEER_TOPK = 16
NORM_EPS = 1e-6

V7X_VMEM_BYTES = 64 * 1024 * 1024
VMEM_LIMIT = V7X_VMEM_BYTES - 4 * 1024 * 1024
V7X_MXU_DIM = 256
SUBLANES = 8
BF16_SUBLANES = 16
LANES = 128


def _params(*sem, flags=None):
    return pltpu.CompilerParams(dimension_semantics=sem, vmem_limit_bytes=VMEM_LIMIT, flags=flags)


CAST_BLOCK_ELEMS = 2 * 1024 * 1024


def _cast_kernel(w_ref, o_ref, *, transpose):
    w = w_ref[...]
    o_ref[...] = (w.T if transpose else w).astype(o_ref.dtype)


def _cast_bf16(w, l, *, transpose=False):
    _, r, c = w.shape
    rows = r
    while rows * c > CAST_BLOCK_ELEMS and rows % 2 == 0 and rows > LANES:
        rows //= 2
    out_shape, out_spec = ((c, r), pl.BlockSpec((c, rows), lambda i: (0, i))) if transpose else (
        (r, c), pl.BlockSpec((rows, c), lambda i: (i, 0)))
    return pl.pallas_call(
        functools.partial(_cast_kernel, transpose=transpose), grid=(r // rows,),
        in_specs=[pl.BlockSpec((None, rows, c), lambda i: (l, i, 0))],
        out_specs=out_spec, out_shape=jax.ShapeDtypeStruct(out_shape, BF16),
        compiler_params=_params("parallel"), name="cast_bf16",
    )(w)


def _addnorm_kernel(*refs, has_delta, emit_sum, emit_norm_f32, emit_norm_t):
    x = refs[0][...]
    k = 1
    if has_delta:
        x = x + refs[k][...].T
        k += 1
    g_ref = refs[k]
    outs = refs[k + 1:]
    y = x * lax.rsqrt(jnp.mean(x * x, axis=-1, keepdims=True) + NORM_EPS)
    y = y * g_ref[...]
    k = 0
    if emit_sum:
        outs[k][...] = x
        k += 1
    if emit_norm_f32:
        outs[k][...] = y
    else:
        outs[k][...] = y.astype(BF16)
    if emit_norm_t:
        outs[k + 1][...] = y.T.astype(BF16)


def _addnorm(x, gain, *, delta_t=None, emit_sum=False, emit_norm_f32=False, emit_norm_t=False,
             rows=256):
    t, d = x.shape
    rows = min(rows, t)
    spec = pl.BlockSpec((rows, d), lambda i: (i, 0))
    spec_t = pl.BlockSpec((d, rows), lambda i: (0, i))
    args, in_specs = [x], [spec]
    if delta_t is not None:
        args.append(delta_t)
        in_specs.append(spec_t)
    args.append(gain.reshape(1, d).astype(F32))
    in_specs.append(pl.BlockSpec((1, d), lambda i: (0, 0)))
    out_shapes, out_specs = [], []
    if emit_sum:
        out_shapes.append(jax.ShapeDtypeStruct((t, d), F32))
        out_specs.append(spec)
    out_shapes.append(jax.ShapeDtypeStruct((t, d), F32 if emit_norm_f32 else BF16))
    out_specs.append(spec)
    if emit_norm_t:
        out_shapes.append(jax.ShapeDtypeStruct((d, t), BF16))
        out_specs.append(spec_t)
    return pl.pallas_call(
        functools.partial(_addnorm_kernel, has_delta=delta_t is not None, emit_sum=emit_sum,
                          emit_norm_f32=emit_norm_f32, emit_norm_t=emit_norm_t),
        grid=(t // rows,), in_specs=in_specs, out_specs=out_specs, out_shape=out_shapes,
        compiler_params=_params("parallel"), name="addnorm",
    )(*args)


def _mm_kernel(x_ref, w_ref, *rest, epilogue, side_transpose):
    n_side = len(side_transpose)
    ins, outs = rest[:len(rest) - 1 - n_side], rest[len(rest) - 1 - n_side:]
    o_ref = outs[0]
    acc = jnp.dot(x_ref[...], w_ref[...], preferred_element_type=F32)
    if epilogue == "sigmoid":
        acc = jax.nn.sigmoid(acc)
    elif epilogue == "residual":
        acc = acc + ins[0][...]
    o_ref[...] = acc.astype(o_ref.dtype)
    for k, transpose in enumerate(side_transpose):
        cw = ins[len(ins) - n_side + k][...]
        outs[1 + k][...] = (cw.T if transpose else cw).astype(outs[1 + k].dtype)


def _matmul(x, w, *, out_dtype, col0=0, n=None, epilogue=None, residual=None, seq=None,
            side_casts=(), tm=1024, tn=1024):
    t, k = x.shape
    n = w.shape[1] if n is None else n
    tm, tn = min(tm, t), min(tn, n)
    assert col0 % tn == 0 and n % tn == 0
    cb = col0 // tn
    nj = n // tn
    if seq is None:
        out_shape, out_spec = (t, n), pl.BlockSpec((tm, tn), lambda i, j: (i, j))
    else:
        tm = min(tm, seq)
        per = seq // tm
        out_shape = (t // seq, seq, n)
        out_spec = pl.BlockSpec((None, tm, tn), lambda i, j: (i // per, i % per, j))
    n_steps = (t // tm) * nj
    in_specs = [pl.BlockSpec((tm, k), lambda i, j: (i, 0)),
                pl.BlockSpec((k, tn), lambda i, j: (0, j + cb))]
    args = [x, w]
    if epilogue == "residual":
        in_specs.append(pl.BlockSpec((tm, tn), lambda i, j: (i, j)))
        args.append(residual)
    out_shapes = [jax.ShapeDtypeStruct(out_shape, out_dtype)]
    out_specs = [out_spec]
    for cw, layer, transpose in side_casts:
        _, r, c = cw.shape
        rows = r // n_steps
        assert r % n_steps == 0 and rows % (LANES if transpose else BF16_SUBLANES) == 0
        in_specs.append(pl.BlockSpec((None, rows, c), lambda i, j, layer=layer: (layer, i * nj + j, 0)))
        args.append(cw)
        if transpose:
            out_shapes.append(jax.ShapeDtypeStruct((c, r), BF16))
            out_specs.append(pl.BlockSpec((c, rows), lambda i, j: (0, i * nj + j)))
        else:
            out_shapes.append(jax.ShapeDtypeStruct((r, c), BF16))
            out_specs.append(pl.BlockSpec((rows, c), lambda i, j: (i * nj + j, 0)))
    outs = pl.pallas_call(
        functools.partial(_mm_kernel, epilogue=epilogue,
                          side_transpose=tuple(tr for _, _, tr in side_casts)),
        grid=(t // tm, nj), in_specs=in_specs, out_specs=out_specs, out_shape=out_shapes,
        compiler_params=_params("parallel", "parallel"), name="matmul",
    )(*args)
    return outs if side_casts else outs[0]


def _glu_kernel(z_ref, wa_ref, wb_ref, o_ref):
    z = z_ref[...]
    a = jnp.dot(z, wa_ref[...], preferred_element_type=F32)
    b = jnp.dot(z, wb_ref[...], preferred_element_type=F32)
    o_ref[...] = (a * jax.nn.sigmoid(b)).astype(o_ref.dtype)


def _glu(z, w_glu, *, tm=1024, tn=512):
    bsz, seq, k = z.shape
    t = bsz * seq
    n = w_glu.shape[1] // 2
    tm, tn = min(tm, seq), min(tn, n)
    per = seq // tm
    nb = n // tn
    return pl.pallas_call(
        _glu_kernel, grid=(t // tm, nb),
        in_specs=[pl.BlockSpec((None, tm, k), lambda i, j: (i // per, i % per, 0)),
                  pl.BlockSpec((k, tn), lambda i, j: (0, j)),
                  pl.BlockSpec((k, tn), lambda i, j: (0, j + nb))],
        out_specs=pl.BlockSpec((tm, tn), lambda i, j: (i, j)),
        out_shape=jax.ShapeDtypeStruct((t, n), BF16),
        compiler_params=_params("parallel", "parallel"), name="glu",
    )(z, w_glu, w_glu)


def _merge_kernel(a_ref, s_ref, ga_ref, gs_ref, wa_ref, ws_ref, o_ref):
    a = jnp.dot(a_ref[...], wa_ref[...], preferred_element_type=F32)
    s = jnp.dot(s_ref[...], ws_ref[...], preferred_element_type=F32)
    o_ref[...] = (ga_ref[...].astype(F32) * a + gs_ref[...].astype(F32) * s).astype(o_ref.dtype)


def _merge(attn, ssm, gates, w_ab, w_sb, *, tm=1024, tn=1024):
    t = attn.shape[0]
    n = w_ab.shape[1]
    tm, tn = min(tm, t), min(tn, n)
    nb = n // tn
    return pl.pallas_call(
        _merge_kernel, grid=(t // tm, nb),
        in_specs=[pl.BlockSpec((tm, attn.shape[1]), lambda i, j: (i, 0)),
                  pl.BlockSpec((tm, ssm.shape[1]), lambda i, j: (i, 0)),
                  pl.BlockSpec((tm, tn), lambda i, j: (i, j)),
                  pl.BlockSpec((tm, tn), lambda i, j: (i, j + nb)),
                  pl.BlockSpec((w_ab.shape[0], tn), lambda i, j: (0, j)),
                  pl.BlockSpec((w_sb.shape[0], tn), lambda i, j: (0, j))],
        out_specs=pl.BlockSpec((tm, tn), lambda i, j: (i, j)),
        out_shape=jax.ShapeDtypeStruct((t, n), BF16),
        compiler_params=_params("parallel", "parallel"), name="merge",
    )(attn, ssm, gates, gates, w_ab, w_sb)


def _alibi_slopes():
    return [float(v) for v in
            np.asarray(2.0 ** (-8.0 * np.arange(1, N_Q_HEADS + 1) / N_Q_HEADS), dtype=np.float32)]


def _attn_kernel(sink_ref, q0_ref, q1_ref, q2_ref, q3_ref, kc_ref, kp_ref, vc_ref, vp_ref, o_ref):
    n = pl.program_id(1)
    qi = lax.broadcasted_iota(jnp.int32, (BLOCK, 2 * BLOCK), 0)
    sj = lax.broadcasted_iota(jnp.int32, (BLOCK, 2 * BLOCK), 1)
    dist = qi - sj + BLOCK
    valid = (dist >= 0) & (dist < WINDOW) & ((n > 0) | (sj >= BLOCK))
    distf = dist.astype(F32)
    slopes = _alibi_slopes()
    scale = HEAD_DIM ** -0.5
    for k, q_ref in enumerate((q0_ref, q1_ref, q2_ref, q3_ref)):
        ksl = slice(k * HEAD_DIM, (k + 1) * HEAD_DIM)
        kw = jnp.concatenate([kp_ref[:, ksl], kc_ref[:, ksl]], axis=0)
        vw = jnp.concatenate([vp_ref[:, ksl], vc_ref[:, ksl]], axis=0)
        q4 = jnp.concatenate([q_ref[:, g * HEAD_DIM:(g + 1) * HEAD_DIM]
                              for g in range(Q_PER_KV)], axis=0)
        logits = lax.dot_general(q4, kw, (((1,), (1,)), ((), ())),
                                 preferred_element_type=F32) * scale
        ps, dens = [], []
        for g in range(Q_PER_KV):
            h = k * Q_PER_KV + g
            lg = logits[g * BLOCK:(g + 1) * BLOCK]
            lg = jnp.where(valid, lg - slopes[h] * distf, NEG_INF)
            sink = sink_ref[h]
            m = jnp.maximum(jnp.max(lg, axis=-1, keepdims=True), sink)
            p = jnp.exp(lg - m)
            dens.append(jnp.sum(p, axis=-1, keepdims=True) + jnp.exp(sink - m))
            ps.append(p.astype(BF16))
        pv = jnp.dot(jnp.concatenate(ps, axis=0), vw, preferred_element_type=F32)
        for g in range(Q_PER_KV):
            h = k * Q_PER_KV + g
            o_ref[:, h * HEAD_DIM:(h + 1) * HEAD_DIM] = (
                pv[g * BLOCK:(g + 1) * BLOCK] / dens[g]).astype(o_ref.dtype)


def _attention(qkv, sinks, *, bsz, seq):
    aw = N_Q_HEADS * HEAD_DIM
    kvw = N_KV_HEADS * HEAD_DIM
    assert aw == N_KV_HEADS * kvw
    nb = seq // BLOCK
    koff, voff = aw // kvw, aw // kvw + 1
    q_specs = [pl.BlockSpec((BLOCK, kvw), lambda b, n, g=g: (b * nb + n, g))
               for g in range(N_KV_HEADS)]
    prev = lambda b, n: b * nb + jnp.maximum(n - 1, 0)
    return pl.pallas_call(
        _attn_kernel, grid=(bsz, nb),
        in_specs=[pl.BlockSpec(memory_space=pltpu.SMEM)] + q_specs + [
            pl.BlockSpec((BLOCK, kvw), lambda b, n: (b * nb + n, koff)),
            pl.BlockSpec((BLOCK, kvw), lambda b, n: (prev(b, n), koff)),
            pl.BlockSpec((BLOCK, kvw), lambda b, n: (b * nb + n, voff)),
            pl.BlockSpec((BLOCK, kvw), lambda b, n: (prev(b, n), voff))],
        out_specs=pl.BlockSpec((BLOCK, aw), lambda b, n: (b * nb + n, 0)),
        out_shape=jax.ShapeDtypeStruct((bsz * seq, aw), BF16),
        compiler_params=_params("parallel", "parallel"), name="swa_attention",
    )(sinks.astype(F32), *([qkv] * (N_KV_HEADS + 4)))


S5_SLICE_GROUPS = 16
S5_SLICE_IN = S5_SLICE_GROUPS * SSM_GROUP
S5_SLICE_STATE = S5_SLICE_GROUPS * SSM_STATE


S5_C_ROWS = 256


def _s5_kernel(u_ref, bd_ref, lam_ref, cd_ref, d_ref, z_ref, bu0, bu1, ui0, ui1, zi_ref, st_ref,
               *, steps):
    c = pl.program_id(1)
    ns = S5_SLICE_STATE
    n_slab = S5_SLICE_IN // LANES
    rows = steps * SUBLANES

    @pl.when(c == 0)
    def _():
        bu1[...] = jnp.zeros_like(bu1)
        ui1[...] = jnp.zeros_like(ui1)

    @pl.when(c <= 1)
    def _():
        st_ref[...] = jnp.zeros_like(st_ref)

    def stage(bu_cur, bu_nxt, ui_cur, ui_nxt):
        for b in range(SUBLANES):
            for sl in range(n_slab):
                ui_nxt[sl, pl.ds(b, steps, stride=SUBLANES), :] = (
                    u_ref[b, :, sl * LANES:(sl + 1) * LANES])
        u_n = jnp.concatenate([ui_nxt[sl] for sl in range(n_slab)], axis=-1)
        bu_nxt[...] = jnp.dot(u_n.astype(BF16), bd_ref[0], preferred_element_type=F32)

        lr = jnp.broadcast_to(lam_ref[0, 0:1, :], (SUBLANES, ns))
        li = jnp.broadcast_to(lam_ref[0, 1:2, :], (SUBLANES, ns))
        xr, xi = st_ref[:, 0:ns], st_ref[:, ns:2 * ns]
        for t in range(steps):
            rs = slice(t * SUBLANES, (t + 1) * SUBLANES)
            nr = lr * xr - li * xi + bu_cur[rs, 0:ns]
            ni = lr * xi + li * xr + bu_cur[rs, ns:2 * ns]
            bu_cur[rs, 0:ns] = nr
            bu_cur[rs, ns:2 * ns] = ni
            xr, xi = nr, ni
        st_ref[:, 0:ns] = xr
        st_ref[:, ns:2 * ns] = xi

        for k in range(rows // S5_C_ROWS):
            rs = slice(k * S5_C_ROWS, (k + 1) * S5_C_ROWS)
            y = jnp.dot(bu_cur[rs, :].astype(BF16), cd_ref[0], preferred_element_type=F32)
            u_c = jnp.concatenate([ui_cur[sl, rs, :] for sl in range(n_slab)], axis=-1)
            z = jax.nn.gelu(y + d_ref[0] * u_c)
            for sl in range(n_slab):
                zi_ref[sl, rs, :] = z[:, sl * LANES:(sl + 1) * LANES]
        for b in range(SUBLANES):
            for sl in range(n_slab):
                z_ref[b, :, sl * LANES:(sl + 1) * LANES] = (
                    zi_ref[sl, pl.ds(b, steps, stride=SUBLANES), :].astype(z_ref.dtype))

    @pl.when(lax.rem(c, 2) == 0)
    def _():
        stage(bu1, bu0, ui1, ui0)

    @pl.when(lax.rem(c, 2) == 1)
    def _():
        stage(bu0, bu1, ui0, ui1)


def _s5_prepare(a_re, a_im, log_step, b_re, b_im, c_re, c_im):
    g = a_re.shape[0]
    ns = g // S5_SLICE_GROUPS
    ar, ai = a_re.astype(F32), a_im.astype(F32)
    dt = jnp.exp(log_step.astype(F32))[:, None]
    mag = jnp.exp(ar * dt)
    lr = mag * jnp.cos(ai * dt)
    li = mag * jnp.sin(ai * dt)
    den = ar * ar + ai * ai
    nr = lr - 1.0
    zr = (nr * ar + li * ai) / den
    zi = (li * ar - nr * ai) / den
    br, bi = b_re.astype(F32), b_im.astype(F32)
    bbar_r = zr[..., None] * br - zi[..., None] * bi
    bbar_i = zr[..., None] * bi + zi[..., None] * br
    eye = jnp.eye(S5_SLICE_GROUPS, dtype=F32)

    def bdiag(m):
        m = m.reshape(ns, S5_SLICE_GROUPS, SSM_STATE, SSM_GROUP)
        return jnp.einsum("ngph,gk->nghkp", m, eye).reshape(ns, S5_SLICE_IN, S5_SLICE_STATE)

    def cdiag(m):
        m = m.reshape(ns, S5_SLICE_GROUPS, SSM_GROUP, SSM_STATE)
        return jnp.einsum("nghp,gk->nkpgh", m, eye).reshape(ns, S5_SLICE_STATE, S5_SLICE_IN)

    bd = jnp.concatenate([bdiag(bbar_r), bdiag(bbar_i)], axis=-1).astype(BF16)
    cd = jnp.concatenate([cdiag(c_re.astype(F32)), -cdiag(c_im.astype(F32))], axis=1).astype(BF16)
    lam = jnp.stack([lr.reshape(ns, S5_SLICE_STATE), li.reshape(ns, S5_SLICE_STATE)], axis=1)
    return bd, lam, cd


def _s5(u, prep, d_skip, *, bsz, seq, steps=128):
    assert u.shape[:2] == (bsz, seq) and bsz == SUBLANES
    bd, lam, cd = prep
    ns = bd.shape[0]
    steps = min(steps, seq)
    rows = steps * bsz
    width = u.shape[2]
    n_chunks = seq // steps
    assert rows % S5_C_ROWS == 0
    in_blk = pl.BlockSpec((bsz, steps, S5_SLICE_IN),
                          lambda j, c: (0, jnp.minimum(c, n_chunks - 1), j))
    out_blk = pl.BlockSpec((bsz, steps, S5_SLICE_IN), lambda j, c: (0, jnp.maximum(c - 1, 0), j))
    slab = pltpu.VMEM((S5_SLICE_IN // LANES, rows, LANES), F32)
    wide = pltpu.VMEM((rows, 2 * S5_SLICE_STATE), F32)
    return pl.pallas_call(
        functools.partial(_s5_kernel, steps=steps),
        grid=(ns, n_chunks + 1),
        in_specs=[in_blk,
                  pl.BlockSpec((1, S5_SLICE_IN, 2 * S5_SLICE_STATE), lambda j, c: (j, 0, 0)),
                  pl.BlockSpec((1, 2, S5_SLICE_STATE), lambda j, c: (j, 0, 0)),
                  pl.BlockSpec((1, 2 * S5_SLICE_STATE, S5_SLICE_IN), lambda j, c: (j, 0, 0)),
                  pl.BlockSpec((1, 1, S5_SLICE_IN), lambda j, c: (j, 0, 0))],
        out_specs=out_blk,
        out_shape=jax.ShapeDtypeStruct((bsz, seq, width), BF16),
        scratch_shapes=[wide, wide, slab, slab, slab,
                        pltpu.VMEM((SUBLANES, 2 * S5_SLICE_STATE), F32)],
        compiler_params=_params("parallel", "arbitrary"), name="s5_scan",
    )(u, bd, lam, cd, d_skip.astype(F32).reshape(ns, 1, S5_SLICE_IN))


ROUTE_KEEP = PEER_TOPK + 1
ROUTE_ROWS = 24
PEER_BLOCK = 512


def _sorting_network(n):
    pairs = []
    p = 1
    while p < n:
        k = p
        while k >= 1:
            for j in range(k % p, n - k, 2 * k):
                for i in range(min(k, n - j - k)):
                    if (i + j) // (2 * p) == (i + j + k) // (2 * p):
                        pairs.append((i + j, i + j + k))
            k //= 2
        p *= 2
    return pairs


def _extract_sorted(s, count):
    n = s.shape[0] // SUBLANES
    v = [s[r * SUBLANES:(r + 1) * SUBLANES] for r in range(n)]
    wires = 1 << (n - 1).bit_length()
    v += [None] * (wires - n)
    for i, j in _sorting_network(wires):
        if v[j] is None:
            continue
        if v[i] is None:
            v[i], v[j] = v[j], None
        else:
            v[i], v[j] = jnp.maximum(v[i], v[j]), jnp.minimum(v[i], v[j])
    v = v[:n]
    sub = lax.broadcasted_iota(jnp.int32, v[0].shape, 0).astype(F32)
    out = []
    for k in range(count):
        m = jnp.max(v[0], axis=0, keepdims=True)
        out.append(m)
        first = jnp.min(jnp.where(v[0] == m, sub, float(SUBLANES)), axis=0, keepdims=True)
        pop = sub == first
        for r in range(min(n, count - k - 1)):
            nxt = v[r + 1] if r + 1 < n else -jnp.inf
            v[r] = jnp.where(pop, nxt, v[r])
    return out


def _route_kernel(q_ref, keys_ref, theta_ref, coef_ref, s2_ref, e2_ref, a_scr, b_scr, *, tt,
                  rows_per_block):
    for h in range(PEER_HEADS):
        subs = []
        for c in range(2):
            qh = q_ref[:, (2 * h + c) * N_KEYS:(2 * h + c + 1) * N_KEYS].astype(BF16)
            subs.append(lax.dot_general(keys_ref[h, c], qh, (((1,), (1,)), ((), ())),
                                        preferred_element_type=F32))
        s1, s2 = subs
        a_scr[...] = jnp.full(a_scr.shape, -jnp.inf, F32)
        b_scr[...] = jnp.full(b_scr.shape, -jnp.inf, F32)
        a_rows = _extract_sorted(s1, ROUTE_KEEP)
        b_rows = _extract_sorted(s2, ROUTE_KEEP)
        for r in range(ROUTE_KEEP):
            a_scr[r:r + 1, :] = a_rows[r]
            b_scr[r:r + 1, :] = b_rows[r]
        cand = [a_rows[0] + b_scr[...]]
        for k in range(1, SUBLANES):
            cand.append(a_rows[k] + b_scr[0:SUBLANES, :])
        cand.append(a_scr[SUBLANES:ROUTE_ROWS, :] + b_rows[0])
        top = _extract_sorted(jnp.concatenate(cand, axis=0), ROUTE_KEEP)
        zsum = jnp.zeros((1, tt), F32)
        for r in range(PEER_TOPK):
            zsum = zsum + jnp.exp(top[r] - top[0])
        thr = 0.5 * (top[PEER_TOPK - 1] + top[PEER_TOPK])
        theta = thr - s1
        coef = jnp.exp(s1 - a_rows[0]) / zsum
        for g in range(N_KEYS // rows_per_block):
            rs = slice(g * rows_per_block, (g + 1) * rows_per_block)
            theta_ref[h, g] = theta[rs]
            coef_ref[h, g] = coef[rs]
        s2_ref[h] = s2
        e2_ref[h] = jnp.exp(s2 - b_rows[0])


def _route(qp, keys, *, rows_per_block, tt=256):
    t = qp.shape[0]
    tt = min(tt, t)
    n_grp = N_KEYS // rows_per_block
    shp = jax.ShapeDtypeStruct((PEER_HEADS, N_KEYS, t), F32)
    ospec = pl.BlockSpec((PEER_HEADS, N_KEYS, tt), lambda i: (0, 0, i))
    shp_i = jax.ShapeDtypeStruct((PEER_HEADS, n_grp, rows_per_block, t), F32)
    ospec_i = pl.BlockSpec((PEER_HEADS, n_grp, rows_per_block, tt), lambda i: (0, 0, 0, i))
    return pl.pallas_call(
        functools.partial(_route_kernel, tt=tt, rows_per_block=rows_per_block),
        grid=(t // tt,),
        in_specs=[pl.BlockSpec((tt, qp.shape[1]), lambda i: (i, 0)),
                  pl.BlockSpec(keys.shape, lambda i: (0, 0, 0, 0))],
        out_specs=[ospec_i, ospec_i, ospec, ospec], out_shape=[shp_i, shp_i, shp, shp],
        scratch_shapes=[pltpu.VMEM((ROUTE_ROWS, tt), F32), pltpu.VMEM((ROUTE_ROWS, tt), F32)],
        compiler_params=_params("parallel"), name="peer_route",
    )(qp, keys)


def _peer_kernel(xt_ref, u_ref, vt_ref, theta_ref, coef_ref, s2_ref, e2_ref, o_ref,
                 ht_scr, ht_next, hg_scr, *, eb, tm, n_blk):
    s = pl.program_id(0)

    @pl.when(s == 0)
    def _():
        ht_scr[...] = jnp.zeros_like(ht_scr)

    @pl.when(lax.rem(jnp.maximum(s - 1, 0), n_blk) == 0)
    def _():
        o_ref[...] = jnp.zeros_like(o_ref)

    n_half = 2
    for c in range(n_half):
        ts = slice(c * (tm // n_half), (c + 1) * (tm // n_half))
        ht_next[:, ts] = jnp.dot(u_ref[...], xt_ref[:, ts], preferred_element_type=F32)
        es = slice(c * (eb // n_half), (c + 1) * (eb // n_half))
        for ii in range(es.start // N_KEYS, es.stop // N_KEYS):
            rs = slice(ii * N_KEYS, (ii + 1) * N_KEYS)
            for lc in range(tm // LANES):
                ls = slice(lc * LANES, (lc + 1) * LANES)
                g = jnp.zeros((N_KEYS, LANES), F32)
                for h in range(PEER_HEADS):
                    th = theta_ref[h, 0, ii:ii + 1, ls]
                    cf = coef_ref[h, 0, ii:ii + 1, ls]
                    g = g + jnp.where(s2_ref[h, :, ls] >= th, e2_ref[h, :, ls] * cf, 0.0)
                hg_scr[rs, ls] = (jax.nn.gelu(ht_scr[rs, ls]) * g).astype(BF16)
        o_ref[...] += jnp.dot(vt_ref[:, es], hg_scr[es, :], preferred_element_type=F32)
    ht_scr[...] = ht_next[...]


def _peer(xnt, u_tab, vt_tab, route, *, tm=512, eb=PEER_BLOCK):
    d, t = xnt.shape
    n_blk = u_tab.shape[0] // eb
    tm = min(tm, t)
    n_i = eb // N_KEYS
    theta, coef, s2, e2 = route
    assert theta.shape[1:3] == (N_KEYS // n_i, n_i)
    n_pairs = (t // tm) * n_blk
    cur = lambda s: jnp.minimum(s, n_pairs - 1)
    prv = lambda s: jnp.maximum(s - 1, 0)
    tile = lambda p: p // n_blk
    blk = lambda p: p % n_blk
    ispec = pl.BlockSpec((PEER_HEADS, 1, n_i, tm), lambda s: (0, blk(prv(s)), 0, tile(prv(s))))
    rspec = pl.BlockSpec((PEER_HEADS, N_KEYS, tm), lambda s: (0, 0, tile(prv(s))))
    return pl.pallas_call(
        functools.partial(_peer_kernel, eb=eb, tm=tm, n_blk=n_blk), grid=(n_pairs + 1,),
        in_specs=[pl.BlockSpec((d, tm), lambda s: (0, tile(cur(s)))),
                  pl.BlockSpec((eb, d), lambda s: (blk(cur(s)), 0)),
                  pl.BlockSpec((d, eb), lambda s: (0, blk(prv(s)))),
                  ispec, ispec, rspec, rspec],
        out_specs=pl.BlockSpec((d, tm), lambda s: (0, tile(prv(s)))),
        out_shape=jax.ShapeDtypeStruct((d, t), F32),
        scratch_shapes=[pltpu.VMEM((eb, tm), F32), pltpu.VMEM((eb, tm), F32),
                        pltpu.VMEM((eb, tm), BF16)],
        compiler_params=_params("arbitrary"), name="peer_experts",
    )(xnt, u_tab, vt_tab, theta, coef, s2, e2)


def _side_casts_fit(t, n, tables, tm=1024, tn=1024):
    n_steps = (t // min(tm, t)) * (n // min(tn, n))
    for table, transpose in tables:
        _, r, c = table.shape
        rows = r // n_steps
        align = LANES if transpose else BF16_SUBLANES
        if r % n_steps or rows % align or rows * c > CAST_BLOCK_ELEMS // 2:
            return False
    return True


def kernel(x, ln_mix, w_in, ssm_a_re, ssm_a_im, ssm_log_step, ssm_b_re, ssm_b_im, ssm_c_re,
           ssm_c_im, ssm_d, w_glu, attn_sinks, w_attn_branch, w_ssm_branch, w_out, ln_ffn,
           w_peer_q, peer_keys, peer_u, peer_v, ln_final):
    bsz, seq, d = x.shape
    depth = w_in.shape[0]
    aw = N_Q_HEADS * HEAD_DIM
    qkv_w = aw + 2 * N_KV_HEADS * HEAD_DIM
    ssm_w = ssm_d.shape[-1]

    h = x.reshape(bsz * seq, d)
    pt = None
    gate_tables = [(peer_u, False), (peer_v, True), (w_out, False), (w_attn_branch, False),
                   (w_peer_q, False)]
    ride_gates = _side_casts_fit(bsz * seq, 2 * d, gate_tables)
    for l in range(depth):
        if l == 0:
            xres = h
            (xn,) = _addnorm(h, ln_mix[l])
        else:
            xres, xn = _addnorm(h, ln_mix[l], delta_t=pt, emit_sum=True)
        w_l = _cast_bf16(w_in, l)
        qkv = _matmul(xn, w_l, n=qkv_w, out_dtype=BF16)
        u = _matmul(xn, w_l, col0=qkv_w, n=ssm_w, out_dtype=F32, seq=seq)
        gates = _matmul(xn, w_l, col0=qkv_w + ssm_w, n=2 * d, out_dtype=BF16, epilogue="sigmoid",
                        side_casts=[(tab, l, tr) for tab, tr in gate_tables] if ride_gates else ())
        if ride_gates:
            gates, u_tab, vt_tab, w_out_l, w_ab_l, w_pq_l = gates
        else:
            u_tab, vt_tab = _cast_bf16(peer_u, l), _cast_bf16(peer_v, l, transpose=True)
            w_out_l, w_ab_l, w_pq_l = (_cast_bf16(w_out, l), _cast_bf16(w_attn_branch, l),
                                       _cast_bf16(w_peer_q, l))
        attn = _attention(qkv, attn_sinks[l], bsz=bsz, seq=seq)
        prep = _s5_prepare(ssm_a_re[l], ssm_a_im[l], ssm_log_step[l], ssm_b_re[l], ssm_b_im[l],
                           ssm_c_re[l], ssm_c_im[l])
        z = _s5(u, prep, ssm_d[l], bsz=bsz, seq=seq)
        ssm = _glu(z, _cast_bf16(w_glu, l))
        merged = _merge(attn, ssm, gates, w_ab_l, _cast_bf16(w_ssm_branch, l))
        h = _matmul(merged, w_out_l, out_dtype=F32, epilogue="residual", residual=xres)
        hn, hnt = _addnorm(h, ln_ffn[l], emit_norm_t=True)
        qp = _matmul(hn, w_pq_l, out_dtype=F32)
        route = _route(qp, peer_keys[l].astype(BF16), rows_per_block=PEER_BLOCK // N_KEYS)
        pt = _peer(hnt, u_tab, vt_tab, route)
    (out,) = _addnorm(h, ln_final, delta_t=pt, emit_norm_f32=True)
    return out.reshape(bsz, seq, d)
```
